```python
import math
import jax, jax.numpy as jnp
from jax import lax
import numpy as np

D_MODEL = 1024
BATCH = 4
SEQ = 8192
DEPTH = 4

N_MIXERS = 4
RMS_EPS = 1e-6
LN_EPS = 1e-5

SSD_EXPAND = 2
SSD_D_INNER = SSD_EXPAND * D_MODEL
SSD_HEAD_DIM = 64
SSD_N_HEADS = SSD_D_INNER // SSD_HEAD_DIM
SSD_N_GROUPS = 8
SSD_HEADS_PER_GROUP = SSD_N_HEADS // SSD_N_GROUPS
SSD_D_STATE = 128
SSD_CONV = 4
SSD_CHUNK = 128
SSD_BC_DIM = SSD_N_GROUPS * SSD_D_STATE
SSD_CONV_DIM = SSD_D_INNER + 2 * SSD_BC_DIM
SSD_IN_DIM = SSD_D_INNER + SSD_CONV_DIM + SSD_N_HEADS

CONF_WIDTH = D_MODEL
CONF_KERNEL = 31

LRU_WIDTH = 1280
LRU_BLOCK = 256
LRU_N_BLOCKS = LRU_WIDTH // LRU_BLOCK
LRU_CONV = 4
LRU_C = 8.0

SGU_CHUNK = 128
SGU_FFN = 4 * D_MODEL
SGU_HALF = SGU_FFN // 2
SGU_GROUPS = 8
SGU_GROUP_DIM = SGU_HALF // SGU_GROUPS

FFN_HIDDEN = 2816
FFN_CONV = 3

kernel_name = 'hybrid_interleaved_ssd_conformer_rglru_gmlp'


def rms_norm(x, g):
    xf = x.astype(jnp.float32)
    y = xf * lax.rsqrt(jnp.mean(xf * xf, axis=-1, keepdims=True) + RMS_EPS)
    return (y * g.astype(jnp.float32)).astype(x.dtype)


def layer_norm(x, g, b):
    xf = x.astype(jnp.float32)
    mu = jnp.mean(xf, axis=-1, keepdims=True)
    xc = xf - mu
    y = xc * lax.rsqrt(jnp.mean(xc * xc, axis=-1, keepdims=True) + LN_EPS)
    return (y * g.astype(jnp.float32) + b.astype(jnp.float32)).astype(x.dtype)


def causal_dwconv(x, w, b):
    k, c = w.shape
    y = lax.conv_general_dilated(
        x, w[:, None, :].astype(x.dtype), window_strides=(1,), padding=[(k - 1, 0)],
        dimension_numbers=('NWC', 'WIO', 'NWC'), feature_group_count=c)
    return y + b.astype(x.dtype)


def ssd_mixer(x, in_proj, conv_w, conv_b, dt_bias, a_log, d_skip, norm_g, out_proj):
    bsz, seqlen, _ = x.shape
    f32 = jnp.float32
    nc = seqlen // SSD_CHUNK
    g, j, p, n, q = SSD_N_GROUPS, SSD_HEADS_PER_GROUP, SSD_HEAD_DIM, SSD_D_STATE, SSD_CHUNK
    zxbcdt = x @ in_proj
    z, xbc, dt = jnp.split(zxbcdt, [SSD_D_INNER, SSD_D_INNER + SSD_CONV_DIM], axis=-1)
    xbc = jax.nn.silu(causal_dwconv(xbc, conv_w, conv_b))
    xs, bm, cm = jnp.split(xbc, [SSD_D_INNER, SSD_D_INNER + SSD_BC_DIM], axis=-1)
    xs = jnp.moveaxis(xs.astype(f32).reshape(bsz, nc, q, g, j, p), 1, 0)
    bm = jnp.moveaxis(bm.astype(f32).reshape(bsz, nc, q, g, n), 1, 0)
    cm = jnp.moveaxis(cm.astype(f32).reshape(bsz, nc, q, g, n), 1, 0)
    dt = jax.nn.softplus(dt.astype(f32) + dt_bias.astype(f32))
    dt = jnp.moveaxis(dt.reshape(bsz, nc, q, g, j), 1, 0)
    a = -jnp.exp(a_log.astype(f32)).reshape(g, j)
    dsk = d_skip.astype(f32).reshape(g, j)
    causal = jnp.tril(jnp.ones((q, q), dtype=bool))[None, :, :, None, None]

    def chunk_step(state, inp):
        xc, bc, cc, dtc = inp
        acs = jnp.cumsum(dtc * a, axis=1)
        seg = acs[:, :, None] - acs[:, None, :]
        decay = jnp.exp(jnp.where(causal, seg, -jnp.inf))
        cb = jnp.einsum('btgn,bsgn->btsg', cc, bc)
        scores = cb[..., None] * decay * dtc[:, None]
        y_diag = jnp.einsum('btsgj,bsgjp->btgjp', scores, xc)
        y_off = jnp.einsum('btgn,bgjpn->btgjp', cc, state) * jnp.exp(acs)[..., None]
        decay_end = jnp.exp(acs[:, -1:] - acs) * dtc
        new_state = (state * jnp.exp(acs[:, -1])[..., None, None]
                     + jnp.einsum('bsgn,bsgj,bsgjp->bgjpn', bc, decay_end, xc))
        return new_state, y_diag + y_off + xc * dsk[..., None]

    state0 = jnp.zeros((bsz, g, j, p, n), f32)
    _, ys = lax.scan(chunk_step, state0, (xs, bm, cm, dt))
    y = jnp.moveaxis(ys, 0, 1).reshape(bsz, seqlen, SSD_D_INNER)
    y = rms_norm(y * jax.nn.silu(z.astype(f32)), norm_g)
    return y.astype(x.dtype) @ out_proj


def conformer_conv(x, pw1_w, pw1_b, dw_w, dw_b, ln_g, ln_b, pw2_w, pw2_b):
    h = jax.nn.glu(x @ pw1_w + pw1_b, axis=-1)
    h = causal_dwconv(h, dw_w, dw_b)
    h = jax.nn.silu(layer_norm(h, ln_g, ln_b))
    return h @ pw2_w + pw2_b


def rglru_block(x, in_w, in_b, conv_w, conv_b, ga_w, ga_b, gx_w, gx_b, lam, out_w, out_b):
    bsz, seqlen, _ = x.shape
    f32 = jnp.float32
    gate, xr = jnp.split(x @ in_w + in_b, 2, axis=-1)
    xr = causal_dwconv(xr, conv_w, conv_b)
    xb = xr.reshape(bsz, seqlen, LRU_N_BLOCKS, LRU_BLOCK)
    r = jax.nn.sigmoid(jnp.einsum('blhi,hij->blhj', xb, ga_w) + ga_b).reshape(bsz, seqlen, LRU_WIDTH)
    i = jax.nn.sigmoid(jnp.einsum('blhi,hij->blhj', xb, gx_w) + gx_b).reshape(bsz, seqlen, LRU_WIDTH)
    log_a = -LRU_C * r.astype(f32) * jax.nn.softplus(-lam.astype(f32))
    a = jnp.exp(log_a)
    bterm = jnp.sqrt(-jnp.expm1(2.0 * log_a)) * (i.astype(f32) * xr.astype(f32))

    def combine(lhs, rhs):
        a1, b1 = lhs
        a2, b2 = rhs
        return a1 * a2, a2 * b1 + b2

    _, h = lax.associative_scan(combine, (a, bterm), axis=1)
    y = jax.nn.gelu(gate) * h.astype(x.dtype)
    return y @ out_w + out_b


def chunked_sgu(x, in_w, in_b, ln_g, ln_b, sp_w, sp_b, out_w, out_b):
    bsz, seqlen, _ = x.shape
    nc = seqlen // SGU_CHUNK
    z = jax.nn.gelu(x @ in_w + in_b)
    u, v = jnp.split(z, 2, axis=-1)
    v = layer_norm(v, ln_g, ln_b).reshape(bsz, nc, SGU_CHUNK, SGU_GROUPS, SGU_GROUP_DIM)
    w = sp_w * jnp.tril(jnp.ones((SGU_CHUNK, SGU_CHUNK), sp_w.dtype))
    mixed = jnp.einsum('gts,bcsgk->bctgk', w, v) + jnp.swapaxes(sp_b, 0, 1)[None, None, :, :, None]
    return (u * mixed.reshape(bsz, seqlen, SGU_HALF)) @ out_w + out_b


def conv_ffn(x, up_w, conv_w, conv_b, down_w):
    h = causal_dwconv(x @ up_w, conv_w, conv_b)
    gt, val = jnp.split(h, 2, axis=-1)
    return (jax.nn.silu(gt) * val) @ down_w


def setup_inputs(seed: int = 0) -> dict:
    key = jax.random.key(seed)
    keys = iter(jax.random.split(key, 64))
    f32 = jnp.float32

    def nrm(shape, scale):
        return jax.random.normal(next(keys), shape, f32) * scale

    def gain(shape):
        return 1.0 + nrm(shape, 0.02)

    def unif(shape, lo, hi):
        return jax.random.uniform(next(keys), shape, f32, minval=lo, maxval=hi)

    n_a = (DEPTH + 3) // N_MIXERS
    n_b = (DEPTH + 2) // N_MIXERS
    n_c = (DEPTH + 1) // N_MIXERS
    n_d = DEPTH // N_MIXERS
    D = D_MODEL

    x = jax.random.normal(next(keys), (BATCH, SEQ, D), f32)
    norm_mix = gain((DEPTH, D))
    norm_ffn = gain((DEPTH, D))
    norm_final = gain((D,))

    a_in_proj = nrm((n_a, D, SSD_IN_DIM), D ** -0.5)
    a_conv_w = nrm((n_a, SSD_CONV, SSD_CONV_DIM), SSD_CONV ** -0.5)
    a_conv_b = nrm((n_a, SSD_CONV_DIM), 0.02)
    dt0 = jnp.exp(unif((n_a, SSD_N_HEADS), math.log(1e-3), math.log(1e-1)))
    a_dt_bias = dt0 + jnp.log(-jnp.expm1(-dt0))
    a_log = jnp.log(unif((n_a, SSD_N_HEADS), 1.0, 16.0))
    a_d_skip = gain((n_a, SSD_N_HEADS))
    a_norm = gain((n_a, SSD_D_INNER))
    a_out_proj = nrm((n_a, SSD_D_INNER, D), SSD_D_INNER ** -0.5)

    b_pw1_w = nrm((n_b, D, 2 * CONF_WIDTH), D ** -0.5)
    b_pw1_b = nrm((n_b, 2 * CONF_WIDTH), 0.02)
    b_dw_w = nrm((n_b, CONF_KERNEL, CONF_WIDTH), CONF_KERNEL ** -0.5)
    b_dw_b = nrm((n_b, CONF_WIDTH), 0.02)
    b_ln_g = gain((n_b, CONF_WIDTH))
    b_ln_b = nrm((n_b, CONF_WIDTH), 0.02)
    b_pw2_w = nrm((n_b, CONF_WIDTH, D), CONF_WIDTH ** -0.5)
    b_pw2_b = nrm((n_b, D), 0.02)

    c_in_w = nrm((n_c, D, 2 * LRU_WIDTH), D ** -0.5)
    c_in_b = nrm((n_c, 2 * LRU_WIDTH), 0.02)
    c_conv_w = nrm((n_c, LRU_CONV, LRU_WIDTH), LRU_CONV ** -0.5)
    c_conv_b = nrm((n_c, LRU_WIDTH), 0.02)
    c_ga_w = nrm((n_c, LRU_N_BLOCKS, LRU_BLOCK, LRU_BLOCK), LRU_BLOCK ** -0.5)
    c_ga_b = nrm((n_c, LRU_N_BLOCKS, LRU_BLOCK), 0.02)
    c_gx_w = nrm((n_c, LRU_N_BLOCKS, LRU_BLOCK, LRU_BLOCK), LRU_BLOCK ** -0.5)
    c_gx_b = nrm((n_c, LRU_N_BLOCKS, LRU_BLOCK), 0.02)
    s = unif((n_c, LRU_WIDTH), 0.9, 0.999) ** (1.0 / LRU_C)
    c_lambda = jnp.log(s) - jnp.log1p(-s)
    c_out_w = nrm((n_c, LRU_WIDTH, D), LRU_WIDTH ** -0.5)
    c_out_b = nrm((n_c, D), 0.02)

    d_in_w = nrm((n_d, D, SGU_FFN), D ** -0.5)
    d_in_b = nrm((n_d, SGU_FFN), 0.02)
    d_ln_g = gain((n_d, SGU_HALF))
    d_ln_b = nrm((n_d, SGU_HALF), 0.02)
    d_sp_w = nrm((n_d, SGU_GROUPS, SGU_CHUNK, SGU_CHUNK), SGU_CHUNK ** -0.5)
    d_sp_b = gain((n_d, SGU_GROUPS, SGU_CHUNK))
    d_out_w = nrm((n_d, SGU_HALF, D), SGU_HALF ** -0.5)
    d_out_b = nrm((n_d, D), 0.02)

    f_up_w = nrm((DEPTH, D, 2 * FFN_HIDDEN), D ** -0.5)
    f_conv_w = nrm((DEPTH, FFN_CONV, 2 * FFN_HIDDEN), FFN_CONV ** -0.5)
    f_conv_b = nrm((DEPTH, 2 * FFN_HIDDEN), 0.02)
    f_down_w = nrm((DEPTH, FFN_HIDDEN, D), FFN_HIDDEN ** -0.5)

    return {'x': x, 'norm_mix': norm_mix, 'norm_ffn': norm_ffn, 'norm_final': norm_final,
            'a_in_proj': a_in_proj, 'a_conv_w': a_conv_w, 'a_conv_b': a_conv_b, 'a_dt_bias': a_dt_bias,
            'a_log': a_log, 'a_d_skip': a_d_skip, 'a_norm': a_norm, 'a_out_proj': a_out_proj,
            'b_pw1_w': b_pw1_w, 'b_pw1_b': b_pw1_b, 'b_dw_w': b_dw_w, 'b_dw_b': b_dw_b,
            'b_ln_g': b_ln_g, 'b_ln_b': b_ln_b, 'b_pw2_w': b_pw2_w, 'b_pw2_b': b_pw2_b,
            'c_in_w': c_in_w, 'c_in_b': c_in_b, 'c_conv_w': c_conv_w, 'c_conv_b': c_conv_b,
            'c_ga_w': c_ga_w, 'c_ga_b': c_ga_b, 'c_gx_w': c_gx_w, 'c_gx_b': c_gx_b,
            'c_lambda': c_lambda, 'c_out_w': c_out_w, 'c_out_b': c_out_b,
            'd_in_w': d_in_w, 'd_in_b': d_in_b, 'd_ln_g': d_ln_g, 'd_ln_b': d_ln_b,
            'd_sp_w': d_sp_w, 'd_sp_b': d_sp_b, 'd_out_w': d_out_w, 'd_out_b': d_out_b,
            'f_up_w': f_up_w, 'f_conv_w': f_conv_w, 'f_conv_b': f_conv_b, 'f_down_w': f_down_w}


def reference(x, norm_mix, norm_ffn, norm_final,
              a_in_proj, a_conv_w, a_conv_b, a_dt_bias, a_log, a_d_skip, a_norm, a_out_proj,
              b_pw1_w, b_pw1_b, b_dw_w, b_dw_b, b_ln_g, b_ln_b, b_pw2_w, b_pw2_b,
              c_in_w, c_in_b, c_conv_w, c_conv_b, c_ga_w, c_ga_b, c_gx_w, c_gx_b, c_lambda, c_out_w, c_out_b,
              d_in_w, d_in_b, d_ln_g, d_ln_b, d_sp_w, d_sp_b, d_out_w, d_out_b,
              f_up_w, f_conv_w, f_conv_b, f_down_w):
    h = x
    for i in range(DEPTH):
        kind, j = i % N_MIXERS, i // N_MIXERS
        u = rms_norm(h, norm_mix[i])
        if kind == 0:
            m = ssd_mixer(u, a_in_proj[j], a_conv_w[j], a_conv_b[j], a_dt_bias[j], a_log[j],
                          a_d_skip[j], a_norm[j], a_out_proj[j])
        elif kind == 1:
            m = conformer_conv(u, b_pw1_w[j], b_pw1_b[j], b_dw_w[j], b_dw_b[j], b_ln_g[j], b_ln_b[j],
                               b_pw2_w[j], b_pw2_b[j])
        elif kind == 2:
            m = rglru_block(u, c_in_w[j], c_in_b[j], c_conv_w[j], c_conv_b[j], c_ga_w[j], c_ga_b[j],
                            c_gx_w[j], c_gx_b[j], c_lambda[j], c_out_w[j], c_out_b[j])
        else:
            m = chunked_sgu(u, d_in_w[j], d_in_b[j], d_ln_g[j], d_ln_b[j], d_sp_w[j], d_sp_b[j],
                            d_out_w[j], d_out_b[j])
        h = h + m
        h = h + conv_ffn(rms_norm(h, norm_ffn[i]), f_up_w[i], f_conv_w[i], f_conv_b[i], f_down_w[i])
    return rms_norm(h, norm_final)
```

```python
import functools
import math

import jax
import jax.numpy as jnp
from jax import lax
from jax.experimental import pallas as pl
from jax.experimental.pallas import tpu as pltpu

F32 = jnp.float32
BF16 = jnp.bfloat16

RMS_EPS = 1e-6
LN_EPS = 1e-5

D_MODEL = 1024

LANES = 128
SUBLANES = 8
VMEM_LIMIT_BYTES = 56 * 1024 * 1024

SSD_D_INNER = 2048
SSD_HEAD_DIM = 64
SSD_N_HEADS = 32
SSD_N_GROUPS = 8
SSD_HEADS_PER_GROUP = 4
SSD_D_STATE = 128
SSD_CONV = 4
SSD_CHUNK = 128
SSD_BC_DIM = 1024
SSD_GROUP_W = SSD_HEADS_PER_GROUP * SSD_HEAD_DIM

CONF_KERNEL = 31
CONF_PAD = 32

LRU_WIDTH = 1280
LRU_BLOCK = 256
LRU_N_BLOCKS = 5
LRU_CONV = 4
LRU_C = 8.0

SGU_CHUNK = 128
SGU_HALF = 2048
SGU_GROUPS = 8
SGU_GROUP_DIM = 256

FFN_HIDDEN = 2816
FFN_CONV = 3
FFN_CHUNK = 256

CONV_PAD = SUBLANES


def _dot(a, b):
    return jnp.dot(a, b, preferred_element_type=F32)


def _dot_nt(a, b):
    return lax.dot_general(a, b, (((1,), (1,)), ((), ())), preferred_element_type=F32)


def _dot_tn(a, b):
    return lax.dot_general(a, b, (((0,), (0,)), ((), ())), preferred_element_type=F32)


def _rms(x, g):
    return x * lax.rsqrt(jnp.mean(x * x, axis=-1, keepdims=True) + RMS_EPS) * g


def _layer_norm(x, g, b):
    mu = jnp.mean(x, axis=-1, keepdims=True)
    xc = x - mu
    return xc * lax.rsqrt(jnp.mean(xc * xc, axis=-1, keepdims=True) + LN_EPS) * g + b


def _sigmoid(x):
    return 1.0 / (1.0 + jnp.exp(-x))


def _silu(x):
    return x * _sigmoid(x)


def _softplus(x):
    return jnp.maximum(x, 0.0) + jnp.log1p(jnp.exp(-jnp.abs(x)))


def _gelu_tanh(x):
    c = math.sqrt(2.0 / math.pi)
    return x * (0.5 * (1.0 + jnp.tanh(c * (x + 0.044715 * (x * x * x)))))


def _split_hi_lo(x):
    hi = x.astype(BF16)
    lo = (x - hi.astype(F32)).astype(BF16)
    return hi, lo


def _causal_conv(y, tail_ref, buf_ref, w, b, taps, tl, pad=CONV_PAD):
    buf_ref[0:pad, :] = tail_ref[...]
    buf_ref[pad:pad + tl, :] = y
    tail_ref[...] = y[tl - pad:tl, :]
    acc = y * w[taps - 1:taps, :] + b
    for k in range(taps - 1):
        off = pad - (taps - 1) + k
        acc = acc + buf_ref[off:off + tl, :] * w[k:k + 1, :]
    return acc


def _wspec(shape):
    nd = len(shape)
    return pl.BlockSpec(shape, lambda b, l: (0,) * nd, pipeline_mode=pl.Buffered(1))


def _tile_spec(tl):
    return pl.BlockSpec((None, tl, D_MODEL), lambda b, l: (b, l, 0))


def _call(body, x, weights, scratch, tl, name):
    bsz, seqlen, _ = x.shape
    assert seqlen % tl == 0, (seqlen, tl)
    return pl.pallas_call(
        body,
        grid=(bsz, seqlen // tl),
        in_specs=[_tile_spec(tl)] + [_wspec(w.shape) for w in weights],
        out_specs=_tile_spec(tl),
        out_shape=jax.ShapeDtypeStruct(x.shape, x.dtype),
        scratch_shapes=scratch,
        compiler_params=pltpu.CompilerParams(
            dimension_semantics=("arbitrary", "arbitrary"),
            vmem_limit_bytes=VMEM_LIMIT_BYTES),
        name=name,
    )(x, *weights)


def _ffn_body(h_ref, ng_ref, upg_ref, upv_ref, cwg_ref, cwv_ref, cbg_ref, cbv_ref, dn_ref, fg_ref,
              o_ref, tail_ref, buf_ref, *, tl, final_norm):
    @pl.when(pl.program_id(1) == 0)
    def _():
        tail_ref[...] = jnp.zeros_like(tail_ref)

    h = h_ref[...]
    u = _rms(h, ng_ref[...]).astype(BF16)
    o_ref[...] = h
    nch = upg_ref.shape[0]
    for j in range(nch):
        parts = []
        for part, (w_ref, cw_ref, cb_ref) in enumerate(
                ((upg_ref, cwg_ref, cbg_ref), (upv_ref, cwv_ref, cbv_ref))):
            y = _dot(u, w_ref[j])
            parts.append(_causal_conv(y, tail_ref.at[part, j], buf_ref.at[(j % 2) * 2 + part],
                                      cw_ref[j], cb_ref[j], FFN_CONV, tl))
        a = (_silu(parts[0]) * parts[1]).astype(BF16)
        o_ref[...] += _dot(a, dn_ref[j])
    if final_norm:
        o_ref[...] = _rms(o_ref[...], fg_ref[...])


def _ffn(h, ng, up_w, conv_w, conv_b, down_w, final_g, *, final_norm, tl):
    hid, hc = FFN_HIDDEN, FFN_CHUNK
    nch = hid // hc

    def cols(w):
        return jnp.transpose(w.reshape(w.shape[0], nch, hc), (1, 0, 2))

    weights = [
        ng.reshape(1, -1),
        cols(up_w[:, :hid]).astype(BF16), cols(up_w[:, hid:]).astype(BF16),
        cols(conv_w[:, :hid]), cols(conv_w[:, hid:]),
        cols(conv_b[None, :hid]), cols(conv_b[None, hid:]),
        down_w.reshape(nch, hc, -1).astype(BF16),
        final_g.reshape(1, -1),
    ]
    scratch = [pltpu.VMEM((2, nch, CONV_PAD, hc), F32), pltpu.VMEM((4, tl + CONV_PAD, hc), F32)]
    body = functools.partial(_ffn_body, tl=tl, final_norm=final_norm)
    return _call(body, h, weights, scratch, tl, "conv_ffn")


def _ssd_body(h_ref, ng_ref, wz_ref, wx_ref, wb_ref, wc_ref, wdt_ref, wdtt_ref,
              cwx_ref, cwb_ref, cwc_ref, cbx_ref, cbb_ref, cbc_ref,
              dtb_r_ref, dtb_c_ref, alog_r_ref, alog_c_ref, dskip_ref, gn_ref, wo_ref,
              expand_ref, tril_ref, triu_ref,
              o_ref,
              tailx_ref, tailb_ref, tailc_ref, bufx_ref, bufb_ref, bufc_ref,
              xs_ref, bs_ref, cs_ref, ys_ref, state_ref, *, tl):
    q = SSD_CHUNK

    @pl.when(pl.program_id(1) == 0)
    def _():
        tailx_ref[...] = jnp.zeros_like(tailx_ref)
        tailb_ref[...] = jnp.zeros_like(tailb_ref)
        tailc_ref[...] = jnp.zeros_like(tailc_ref)
        state_ref[...] = jnp.zeros_like(state_ref)

    h = h_ref[...]
    u = _rms(h, ng_ref[...]).astype(BF16)

    xs_ref[...] = _silu(_causal_conv(_dot(u, wx_ref[...]), tailx_ref, bufx_ref,
                                     cwx_ref[...], cbx_ref[...], SSD_CONV, tl))
    bs_ref[...] = _silu(_causal_conv(_dot(u, wb_ref[...]), tailb_ref, bufb_ref,
                                     cwb_ref[...], cbb_ref[...], SSD_CONV, tl)).astype(BF16)
    cs_ref[...] = _silu(_causal_conv(_dot(u, wc_ref[...]), tailc_ref, bufc_ref,
                                     cwc_ref[...], cbc_ref[...], SSD_CONV, tl)).astype(BF16)

    head_r = lax.broadcasted_iota(jnp.int32, (1, LANES), 1) < SSD_N_HEADS
    head_c = lax.broadcasted_iota(jnp.int32, (LANES, 1), 0) < SSD_N_HEADS
    a_r = jnp.where(head_r, -jnp.exp(alog_r_ref[...]), 0.0)
    a_c = jnp.where(head_c, -jnp.exp(alog_c_ref[...]), 0.0)
    dt_all = _softplus(_dot(u, wdt_ref[...]) + dtb_r_ref[...])
    dtt_all = _softplus(_dot_nt(wdtt_ref[...], u) + dtb_c_ref[...])

    causal = (lax.broadcasted_iota(jnp.int32, (q, q), 0) >= lax.broadcasted_iota(jnp.int32, (q, q), 1))
    lane_head = lax.broadcasted_iota(jnp.int32, (1, SSD_GROUP_W), 1) // SSD_HEAD_DIM
    row16 = lax.broadcasted_iota(jnp.int32, (16, LANES), 0)
    expand = expand_ref[...]
    tril = tril_ref[...]
    triu = triu_ref[...]
    dskip = dskip_ref[...]

    for c in range(tl // q):
        rows = slice(c * q, (c + 1) * q)
        dt = dt_all[rows, :]
        dtt = dtt_all[:, rows]
        hi, lo = _split_hi_lo(dt * a_r)
        acs = _dot(tril, hi) + _dot(tril, lo)
        hi_t, lo_t = _split_hi_lo(dtt * a_c)
        acs_t = _dot(hi_t, triu) + _dot(lo_t, triu)
        eacs = jnp.exp(acs)
        acs_last = acs[q - 1:q, :]
        dend = jnp.exp(acs_last - acs) * dt
        el_hi, el_lo = _split_hi_lo(jnp.exp(acs_last))
        el = jnp.where(row16 == 0, el_hi.astype(F32), jnp.where(row16 == 1, el_lo.astype(F32), 0.0))
        el_x = _dot(el.astype(BF16), expand)
        elast_x = el_x[0:1, :] + el_x[1:2, :]
        dend_x = _dot(dend.astype(BF16), expand)

        x_c = xs_ref[rows, :]
        x_b = x_c.astype(BF16)
        xdd = (x_c * dend_x).astype(BF16)
        b_c = bs_ref[rows, :]
        c_c = cs_ref[rows, :]
        for g in range(SSD_N_GROUPS):
            gs = slice(g * SSD_GROUP_W, (g + 1) * SSD_GROUP_W)
            ns = slice(g * SSD_D_STATE, (g + 1) * SSD_D_STATE)
            b_g = b_c[:, ns]
            c_g = c_c[:, ns]
            cb = _dot_nt(c_g, b_g)
            st = state_ref[:, gs]
            st_b = st.astype(BF16)
            x_g = x_b[:, gs]
            c_g32 = c_g.astype(F32)
            y_g = x_c[:, gs] * dskip[:, gs]
            for j in range(SSD_HEADS_PER_GROUP):
                hd = g * SSD_HEADS_PER_GROUP + j
                seg = acs[:, hd:hd + 1] - acs_t[hd:hd + 1, :]
                decay = jnp.exp(jnp.where(causal, seg, -jnp.inf))
                scores = (cb * decay * dtt[hd:hd + 1, :]).astype(BF16)
                c_scaled = (c_g32 * eacs[:, hd:hd + 1]).astype(BF16)
                sel = lane_head == j
                lhs = jnp.concatenate([scores, c_scaled], axis=1)
                rhs = jnp.concatenate([jnp.where(sel, x_g, jnp.zeros_like(x_g)),
                                       jnp.where(sel, st_b, jnp.zeros_like(st_b))], axis=0)
                y_g = y_g + _dot(lhs, rhs)
            ys_ref[rows, gs] = y_g
            state_ref[:, gs] = st * elast_x[:, gs] + _dot_tn(b_g, xdd[:, gs])

    z = _dot(u, wz_ref[...])
    y = _rms(ys_ref[...] * _silu(z), gn_ref[...]).astype(BF16)
    o_ref[...] = h + _dot(y, wo_ref[...])


def _ssd(h, ng, in_proj, conv_w, conv_b, dt_bias, a_log, d_skip, norm_g, out_proj, *, tl):
    di, bc, nh = SSD_D_INNER, SSD_BC_DIM, SSD_N_HEADS
    o_x, o_b, o_c, o_dt = di, 2 * di, 2 * di + bc, 2 * di + 2 * bc

    def pad_heads(v):
        return jnp.pad(v, (0, LANES - nh))

    w_dt = jnp.pad(in_proj[:, o_dt:], ((0, 0), (0, LANES - nh)))
    head_of_col = jnp.arange(di, dtype=jnp.int32) // SSD_HEAD_DIM
    expand = (jnp.arange(LANES, dtype=jnp.int32)[:, None] == head_of_col[None, :]).astype(BF16)
    ii = jnp.arange(SSD_CHUNK, dtype=jnp.int32)
    tril = (ii[:, None] >= ii[None, :]).astype(BF16)
    weights = [
        ng.reshape(1, -1),
        in_proj[:, :o_x].astype(BF16), in_proj[:, o_x:o_b].astype(BF16),
        in_proj[:, o_b:o_c].astype(BF16), in_proj[:, o_c:o_dt].astype(BF16),
        w_dt.astype(BF16), w_dt.T.astype(BF16),
        conv_w[:, :di], conv_w[:, di:di + bc], conv_w[:, di + bc:],
        conv_b[None, :di], conv_b[None, di:di + bc], conv_b[None, di + bc:],
        pad_heads(dt_bias)[None, :], pad_heads(dt_bias)[:, None],
        pad_heads(a_log)[None, :], pad_heads(a_log)[:, None],
        jnp.repeat(d_skip, SSD_HEAD_DIM)[None, :],
        norm_g.reshape(1, -1), out_proj.astype(BF16),
        expand, tril, tril.T,
    ]
    pad = CONV_PAD
    scratch = [
        pltpu.VMEM((pad, di), F32), pltpu.VMEM((pad, bc), F32), pltpu.VMEM((pad, bc), F32),
        pltpu.VMEM((tl + pad, di), F32), pltpu.VMEM((tl + pad, bc), F32), pltpu.VMEM((tl + pad, bc), F32),
        pltpu.VMEM((tl, di), F32), pltpu.VMEM((tl, bc), BF16), pltpu.VMEM((tl, bc), BF16),
        pltpu.VMEM((tl, di), F32), pltpu.VMEM((SSD_D_STATE, di), F32),
    ]
    return _call(functools.partial(_ssd_body, tl=tl), h, weights, scratch, tl, "ssd_mixer")


def _conf_body(h_ref, ng_ref, w1a_ref, w1b_ref, b1a_ref, b1b_ref, dww_ref, dwb_ref, lng_ref, lnb_ref,
               w2_ref, b2_ref, o_ref, buf_ref, cv_ref, *, tl):
    pad = CONF_PAD

    @pl.when(pl.program_id(1) == 0)
    def _():
        buf_ref[0:pad, :] = jnp.zeros((pad, D_MODEL), F32)

    h = h_ref[...]
    u = _rms(h, ng_ref[...]).astype(BF16)
    glu = (_dot(u, w1a_ref[...]) + b1a_ref[...]) * _sigmoid(_dot(u, w1b_ref[...]) + b1b_ref[...])
    buf_ref[pad:pad + tl, :] = glu

    rb = 128
    for r0 in range(0, tl, rb):
        for c0 in range(0, D_MODEL, LANES):
            cs = slice(c0, c0 + LANES)
            acc = jnp.zeros((rb, LANES), F32) + dwb_ref[:, cs]
            for k in range(CONF_KERNEL):
                off = pad - (CONF_KERNEL - 1) + k + r0
                acc = acc + buf_ref[off:off + rb, cs] * dww_ref[k:k + 1, cs]
            cv_ref[r0:r0 + rb, cs] = acc
    buf_ref[0:pad, :] = buf_ref[tl:tl + pad, :]

    v = _silu(_layer_norm(cv_ref[...], lng_ref[...], lnb_ref[...])).astype(BF16)
    o_ref[...] = h + _dot(v, w2_ref[...]) + b2_ref[...]


def _conformer(h, ng, pw1_w, pw1_b, dw_w, dw_b, ln_g, ln_b, pw2_w, pw2_b, *, tl):
    d = D_MODEL
    weights = [
        ng.reshape(1, -1),
        pw1_w[:, :d].astype(BF16), pw1_w[:, d:].astype(BF16), pw1_b[None, :d], pw1_b[None, d:],
        dw_w, dw_b[None, :], ln_g[None, :], ln_b[None, :],
        pw2_w.astype(BF16), pw2_b[None, :],
    ]
    scratch = [pltpu.VMEM((tl + CONF_PAD, d), F32), pltpu.VMEM((tl, d), F32)]
    return _call(functools.partial(_conf_body, tl=tl), h, weights, scratch, tl, "conformer_conv")


def _lru_body(h_ref, ng_ref, wg_ref, wx_ref, bg_ref, bx_ref, cw_ref, cb_ref, gaw_ref, gab_ref,
              gxw_ref, gxb_ref, lam_ref, wo_ref, bo_ref, o_ref,
              tail_ref, buf_ref, hprev_ref, sa_ref, sb_ref, *, tl):
    spad = tl // 2

    @pl.when(pl.program_id(1) == 0)
    def _():
        tail_ref[...] = jnp.zeros_like(tail_ref)
        hprev_ref[...] = jnp.zeros_like(hprev_ref)
        sa_ref[0:spad, :] = jnp.ones((spad, LRU_WIDTH), F32)
        sb_ref[0:spad, :] = jnp.zeros((spad, LRU_WIDTH), F32)

    h = h_ref[...]
    u = _rms(h, ng_ref[...]).astype(BF16)
    gate = _dot(u, wg_ref[...]) + bg_ref[...]
    xr = _causal_conv(_dot(u, wx_ref[...]) + bx_ref[...], tail_ref, buf_ref,
                      cw_ref[...], cb_ref[...], LRU_CONV, tl)
    r_parts, i_parts = [], []
    for k in range(LRU_N_BLOCKS):
        xb = xr[:, k * LRU_BLOCK:(k + 1) * LRU_BLOCK].astype(BF16)
        r_parts.append(_sigmoid(_dot(xb, gaw_ref[k]) + gab_ref[k]))
        i_parts.append(_sigmoid(_dot(xb, gxw_ref[k]) + gxb_ref[k]))
    r = jnp.concatenate(r_parts, axis=1)
    i = jnp.concatenate(i_parts, axis=1)
    log_a = (-LRU_C) * r * _softplus(-lam_ref[...])
    a = jnp.exp(log_a)
    b = jnp.sqrt(jnp.tanh(-log_a) * (1.0 + a * a)) * (i * xr)

    row = lax.broadcasted_iota(jnp.int32, (tl, 1), 0)
    b = b + jnp.where(row == 0, a * hprev_ref[...], 0.0)
    s = 1
    while s < tl:
        sa_ref[spad:spad + tl, :] = a
        sb_ref[spad:spad + tl, :] = b
        b = a * sb_ref[spad - s:spad - s + tl, :] + b
        if 2 * s < tl:
            a = a * sa_ref[spad - s:spad - s + tl, :]
        s *= 2
    hprev_ref[...] = b[tl - 1:tl, :]

    y = (_gelu_tanh(gate) * b).astype(BF16)
    o_ref[...] = h + _dot(y, wo_ref[...]) + bo_ref[...]


def _rglru(h, ng, in_w, in_b, conv_w, conv_b, ga_w, ga_b, gx_w, gx_b, lam, out_w, out_b, *, tl):
    w = LRU_WIDTH
    weights = [
        ng.reshape(1, -1),
        in_w[:, :w].astype(BF16), in_w[:, w:].astype(BF16), in_b[None, :w], in_b[None, w:],
        conv_w, conv_b[None, :],
        ga_w.astype(BF16), ga_b[:, None, :], gx_w.astype(BF16), gx_b[:, None, :],
        lam[None, :], out_w.astype(BF16), out_b[None, :],
    ]
    scratch = [
        pltpu.VMEM((CONV_PAD, w), F32), pltpu.VMEM((tl + CONV_PAD, w), F32), pltpu.VMEM((1, w), F32),
        pltpu.VMEM((tl + tl // 2, w), F32), pltpu.VMEM((tl + tl // 2, w), F32),
    ]
    return _call(functools.partial(_lru_body, tl=tl), h, weights, scratch, tl, "rglru_block")


def _sgu_body(h_ref, ng_ref, wu_ref, wv_ref, bu_ref, bv_ref, lng_ref, lnb_ref, spw_ref, spb_ref,
              wo_ref, bo_ref, o_ref, zu_ref, vn_ref, gt_ref, *, tl):
    q = SGU_CHUNK
    h = h_ref[...]
    u = _rms(h, ng_ref[...]).astype(BF16)
    zu_ref[...] = _gelu_tanh(_dot(u, wu_ref[...]) + bu_ref[...])
    v = _gelu_tanh(_dot(u, wv_ref[...]) + bv_ref[...])
    vn_ref[...] = _layer_norm(v, lng_ref[...], lnb_ref[...]).astype(BF16)

    causal = (lax.broadcasted_iota(jnp.int32, (q, q), 0) >= lax.broadcasted_iota(jnp.int32, (q, q), 1))
    spb = spb_ref[...]
    for g in range(SGU_GROUPS):
        gs = slice(g * SGU_GROUP_DIM, (g + 1) * SGU_GROUP_DIM)
        w = jnp.where(causal, spw_ref[g], 0.0).astype(BF16)
        bias = spb[:, g:g + 1]
        for c in range(tl // q):
            rows = slice(c * q, (c + 1) * q)
            mixed = _dot(w, vn_ref[rows, gs]) + bias
            gt_ref[rows, gs] = (zu_ref[rows, gs] * mixed).astype(BF16)
    o_ref[...] = h + _dot(gt_ref[...], wo_ref[...]) + bo_ref[...]


def _sgu(h, ng, in_w, in_b, ln_g, ln_b, sp_w, sp_b, out_w, out_b, *, tl):
    hf = SGU_HALF
    weights = [
        ng.reshape(1, -1),
        in_w[:, :hf].astype(BF16), in_w[:, hf:].astype(BF16), in_b[None, :hf], in_b[None, hf:],
        ln_g[None, :], ln_b[None, :], sp_w, sp_b.T,
        out_w.astype(BF16), out_b[None, :],
    ]
    scratch = [pltpu.VMEM((tl, hf), F32), pltpu.VMEM((tl, hf), BF16), pltpu.VMEM((tl, hf), BF16)]
    return _call(functools.partial(_sgu_body, tl=tl), h, weights, scratch, tl, "chunked_sgu")


def _tile(seqlen, want):
    return min(want, seqlen)


def kernel(x, norm_mix, norm_ffn, norm_final, a_in_proj, a_conv_w, a_conv_b, a_dt_bias, a_log, a_d_skip, a_norm, a_out_proj, b_pw1_w, b_pw1_b, b_dw_w, b_dw_b, b_ln_g, b_ln_b, b_pw2_w, b_pw2_b, c_in_w, c_in_b, c_conv_w, c_conv_b, c_ga_w, c_ga_b, c_gx_w, c_gx_b, c_lambda, c_out_w, c_out_b, d_in_w, d_in_b, d_ln_g, d_ln_b, d_sp_w, d_sp_b, d_out_w, d_out_b, f_up_w, f_conv_w, f_conv_b, f_down_w):
    depth = norm_mix.shape[0]
    seqlen = x.shape[1]
    h = x
    for i in range(depth):
        kind, j = i % 4, i // 4
        if kind == 0:
            h = _ssd(h, norm_mix[i], a_in_proj[j], a_conv_w[j], a_conv_b[j], a_dt_bias[j], a_log[j],
                     a_d_skip[j], a_norm[j], a_out_proj[j], tl=_tile(seqlen, 256))
        elif kind == 1:
            h = _conformer(h, norm_mix[i], b_pw1_w[j], b_pw1_b[j], b_dw_w[j], b_dw_b[j], b_ln_g[j],
                           b_ln_b[j], b_pw2_w[j], b_pw2_b[j], tl=_tile(seqlen, 512))
        elif kind == 2:
            h = _rglru(h, norm_mix[i], c_in_w[j], c_in_b[j], c_conv_w[j], c_conv_b[j], c_ga_w[j],
                       c_ga_b[j], c_gx_w[j], c_gx_b[j], c_lambda[j], c_out_w[j], c_out_b[j],
                       tl=_tile(seqlen, 256))
        else:
            h = _sgu(h, norm_mix[i], d_in_w[j], d_in_b[j], d_ln_g[j], d_ln_b[j], d_sp_w[j], d_sp_b[j],
                     d_out_w[j], d_out_b[j], tl=_tile(seqlen, 512))
        h = _ffn(h, norm_ffn[i], f_up_w[i], f_conv_w[i], f_conv_b[i], f_down_w[i], norm_final,
                 final_norm=(i == depth - 1), tl=_tile(seqlen, 512))
    return h
```

```python
import functools
import math

import jax
import jax.numpy as jnp
from jax import lax
from jax.experimental import pallas as pl
from jax.experimental.pallas import tpu as pltpu

F32 = jnp.float32
BF16 = jnp.bfloat16

RMS_EPS = 1e-6
LN_EPS = 1e-5

D_MODEL = 1024

LANES = 128
SUBLANES = 8
VMEM_LIMIT_BYTES = 56 * 1024 * 1024
SSD_D_INNER = 2048
SSD_HEAD_DIM = 64
SSD_N_HEADS = 32
SSD_N_GROUPS = 8
SSD_HEADS_PER_GROUP = 4
SSD_D_STATE = 128
SSD_CONV = 4
SSD_CHUNK = 128
SSD_BC_DIM = 1024
SSD_GROUP_W = SSD_HEADS_PER_GROUP * SSD_HEAD_DIM

CONF_KERNEL = 31
CONF_PAD = 32
CONF_CHUNK = 256
CONF_ROW_BLOCK = 32

LRU_WIDTH = 1280
LRU_BLOCK = 256
LRU_N_BLOCKS = 5
LRU_CONV = 4
LRU_C = 8.0

SGU_CHUNK = 128
SGU_HALF = 2048
SGU_GROUPS = 8
SGU_GROUP_DIM = 256

FFN_HIDDEN = 2816
FFN_CONV = 3
FFN_CHUNK = 256
FFN_ROW_BLOCK = 32

CONV_PAD = SUBLANES


def _dot(a, b):
    return jnp.dot(a, b, preferred_element_type=F32)


def _dot_nt(a, b):
    return lax.dot_general(a, b, (((1,), (1,)), ((), ())), preferred_element_type=F32)


def _dot_tn(a, b):
    return lax.dot_general(a, b, (((0,), (0,)), ((), ())), preferred_element_type=F32)


def _rms(x, g):
    return x * lax.rsqrt(jnp.mean(x * x, axis=-1, keepdims=True) + RMS_EPS) * g


def _layer_norm(x, g, b):
    mu = jnp.mean(x, axis=-1, keepdims=True)
    xc = x - mu
    return xc * lax.rsqrt(jnp.mean(xc * xc, axis=-1, keepdims=True) + LN_EPS) * g + b


def _sigmoid(x):
    return 1.0 / (1.0 + jnp.exp(-x))


def _silu(x):
    return x * _sigmoid(x)


def _softplus(x):
    return jnp.maximum(x, 0.0) + jnp.log1p(jnp.exp(-jnp.abs(x)))


def _gelu_tanh(x):
    c = math.sqrt(2.0 / math.pi)
    return x * (0.5 * (1.0 + jnp.tanh(c * (x + 0.044715 * (x * x * x)))))


def _split_hi_lo(x):
    hi = x.astype(BF16)
    lo = (x - hi.astype(F32)).astype(BF16)
    return hi, lo


def _shift_rows(y, tail, s):
    rolled = pltpu.roll(y, s, axis=0)
    row = lax.broadcasted_iota(jnp.int32, (SUBLANES, 1), 0)
    head = jnp.where(row < s, pltpu.roll(tail, s, axis=0), rolled[0:SUBLANES, :])
    return jnp.concatenate([head, rolled[SUBLANES:, :]], axis=0)


def _causal_conv(y, tail_ref, w, b, taps, tl):
    tail = tail_ref[...]
    tail_ref[...] = y[tl - CONV_PAD:tl, :]
    acc = y * w[taps - 1:taps, :] + b
    for k in range(taps - 1):
        acc = acc + _shift_rows(y, tail, taps - 1 - k) * w[k:k + 1, :]
    return acc


def _wspec(shape):
    nd = len(shape)
    return pl.BlockSpec(shape, lambda b, l: (0,) * nd, pipeline_mode=pl.Buffered(1))


def _tile_spec(tl):
    return pl.BlockSpec((None, tl, D_MODEL), lambda b, l: (b, l, 0))


def _call(body, x, weights, scratch, tl, name):
    bsz, seqlen, _ = x.shape
    assert seqlen % tl == 0, (seqlen, tl)
    return pl.pallas_call(
        body,
        grid=(bsz, seqlen // tl),
        in_specs=[_tile_spec(tl)] + [_wspec(w.shape) for w in weights],
        out_specs=_tile_spec(tl),
        out_shape=jax.ShapeDtypeStruct(x.shape, x.dtype),
        scratch_shapes=scratch,
        compiler_params=pltpu.CompilerParams(
            dimension_semantics=("arbitrary", "arbitrary"),
            vmem_limit_bytes=VMEM_LIMIT_BYTES),
        name=name,
    )(x, *weights)


def _ffn_body(h_ref, ng_ref, upg_ref, upv_ref, cwg_ref, cwv_ref, cbg_ref, cbv_ref, dn_ref, fg_ref,
              o_ref, tail_ref, yg0_ref, yv0_ref, yg1_ref, yv1_ref, u_ref, a_ref, *, tl, final_norm):
    pad, taps, rb = CONV_PAD, FFN_CONV, FFN_ROW_BLOCK

    @pl.when(pl.program_id(1) == 0)
    def _():
        tail_ref[...] = jnp.zeros_like(tail_ref)

    u_ref[...] = _rms(h_ref[...], ng_ref[...]).astype(BF16)
    nch = upg_ref.shape[0]
    even_bufs, odd_bufs = (yg0_ref, yv0_ref), (yg1_ref, yv1_ref)

    hc = upg_ref.shape[2]
    nhalf = hc // LANES

    def up_chunk(j, bufs):
        for part, w_ref in enumerate((upg_ref, upv_ref)):
            y = _dot(u_ref[...], w_ref[j])
            for hf in range(nhalf):
                cs = slice(hf * LANES, (hf + 1) * LANES)
                bufs[part][hf, 0:pad, :] = tail_ref[part, j, :, cs]
                bufs[part][hf, pad:pad + tl, :] = y[:, cs]
                tail_ref[part, j, :, cs] = bufs[part][hf, tl:tl + pad, :]

    def act_chunk(j, bufs):
        cws = (cwg_ref[j], cwv_ref[j])
        cbs = (cbg_ref[j], cbv_ref[j])
        for hf in range(nhalf):
            cs = slice(hf * LANES, (hf + 1) * LANES)
            for r0 in range(0, tl, rb):
                conv = []
                for part in range(2):
                    acc = cbs[part][:, cs]
                    for k in range(taps):
                        off = pad - (taps - 1) + k + r0
                        acc = acc + bufs[part][hf, off:off + rb, :] * cws[part][k:k + 1, cs]
                    conv.append(acc)
                a_ref[j, r0:r0 + rb, cs] = (_silu(conv[0]) * conv[1]).astype(BF16)

    for j in range(nch):
        bufs = (even_bufs, odd_bufs)[j % 2]
        up_chunk(j, bufs)
        act_chunk(j, bufs)

    out = h_ref[...]
    for j in range(nch):
        out = out + _dot(a_ref[j], dn_ref[j])
    if final_norm:
        out = _rms(out, fg_ref[...])
    o_ref[...] = out


def _ffn(h, ng, up_w, conv_w, conv_b, down_w, final_g, *, final_norm, tl):
    hid, hc = FFN_HIDDEN, FFN_CHUNK
    nch = hid // hc

    def cols(w):
        return jnp.transpose(w.reshape(w.shape[0], nch, hc), (1, 0, 2))

    weights = [
        ng.reshape(1, -1),
        cols(up_w[:, :hid]).astype(BF16), cols(up_w[:, hid:]).astype(BF16),
        cols(conv_w[:, :hid]), cols(conv_w[:, hid:]),
        cols(conv_b[None, :hid]), cols(conv_b[None, hid:]),
        down_w.reshape(nch, hc, -1).astype(BF16),
        final_g.reshape(1, -1),
    ]
    scratch = ([pltpu.VMEM((2, nch, CONV_PAD, hc), F32)]
               + [pltpu.VMEM((hc // LANES, tl + CONV_PAD, LANES), F32)] * 4
               + [pltpu.VMEM((tl, D_MODEL), BF16), pltpu.VMEM((nch, tl, hc), BF16)])
    body = functools.partial(_ffn_body, tl=tl, final_norm=final_norm)
    return _call(body, h, weights, scratch, tl, "conv_ffn")


def _ssd_body(h_ref, ng_ref, wz_ref, wx_ref, wb_ref, wc_ref, wdt_ref, wdtt_ref,
              cwx_ref, cwb_ref, cwc_ref, cbx_ref, cbb_ref, cbc_ref,
              dtb_r_ref, dtb_c_ref, alog_r_ref, alog_c_ref, dskip_ref, gn_ref, wo_ref,
              expand_ref, tril_ref, triu_ref,
              o_ref,
              tailx_ref, tailb_ref, tailc_ref,
              xs_ref, bs_ref, cs_ref, ys_ref, state_ref, *, tl):
    q = SSD_CHUNK

    @pl.when(pl.program_id(1) == 0)
    def _():
        tailx_ref[...] = jnp.zeros_like(tailx_ref)
        tailb_ref[...] = jnp.zeros_like(tailb_ref)
        tailc_ref[...] = jnp.zeros_like(tailc_ref)
        state_ref[...] = jnp.zeros_like(state_ref)

    h = h_ref[...]
    u = _rms(h, ng_ref[...]).astype(BF16)

    xs_ref[...] = _silu(_causal_conv(_dot(u, wx_ref[...]), tailx_ref,
                                     cwx_ref[...], cbx_ref[...], SSD_CONV, tl))
    bs_ref[...] = _silu(_causal_conv(_dot(u, wb_ref[...]), tailb_ref,
                                     cwb_ref[...], cbb_ref[...], SSD_CONV, tl)).astype(BF16)
    cs_ref[...] = _silu(_causal_conv(_dot(u, wc_ref[...]), tailc_ref,
                                     cwc_ref[...], cbc_ref[...], SSD_CONV, tl)).astype(BF16)

    head_r = lax.broadcasted_iota(jnp.int32, (1, LANES), 1) < SSD_N_HEADS
    head_c = lax.broadcasted_iota(jnp.int32, (LANES, 1), 0) < SSD_N_HEADS
    a_r = jnp.where(head_r, -jnp.exp(alog_r_ref[...]), 0.0)
    a_c = jnp.where(head_c, -jnp.exp(alog_c_ref[...]), 0.0)
    dt_all = _softplus(_dot(u, wdt_ref[...]) + dtb_r_ref[...])
    dtt_all = _softplus(_dot_nt(wdtt_ref[...], u) + dtb_c_ref[...])

    causal = (lax.broadcasted_iota(jnp.int32, (q, q), 0) >= lax.broadcasted_iota(jnp.int32, (q, q), 1))
    lane_head = lax.broadcasted_iota(jnp.int32, (1, SSD_GROUP_W), 1) // SSD_HEAD_DIM
    row16 = lax.broadcasted_iota(jnp.int32, (16, LANES), 0)
    expand = expand_ref[...]
    tril = tril_ref[...]
    triu = triu_ref[...]
    dskip = dskip_ref[...]

    for c in range(tl // q):
        rows = slice(c * q, (c + 1) * q)
        dt = dt_all[rows, :]
        dtt = dtt_all[:, rows]
        hi, lo = _split_hi_lo(dt * a_r)
        acs = _dot(tril, hi) + _dot(tril, lo)
        hi_t, lo_t = _split_hi_lo(dtt * a_c)
        acs_t = _dot(hi_t, triu) + _dot(lo_t, triu)
        eacs = jnp.exp(acs)
        acs_last = acs[q - 1:q, :]
        dend = jnp.exp(acs_last - acs) * dt
        el_hi, el_lo = _split_hi_lo(jnp.exp(acs_last))
        el = jnp.where(row16 == 0, el_hi.astype(F32), jnp.where(row16 == 1, el_lo.astype(F32), 0.0))
        el_x = _dot(el.astype(BF16), expand)
        elast_x = el_x[0:1, :] + el_x[1:2, :]
        dend_x = _dot(dend.astype(BF16), expand)

        x_c = xs_ref[rows, :]
        x_b = x_c.astype(BF16)
        xdd = (x_c * dend_x).astype(BF16)
        b_c = bs_ref[rows, :]
        c_c = cs_ref[rows, :]
        for g in range(SSD_N_GROUPS):
            gs = slice(g * SSD_GROUP_W, (g + 1) * SSD_GROUP_W)
            ns = slice(g * SSD_D_STATE, (g + 1) * SSD_D_STATE)
            b_g = b_c[:, ns]
            c_g = c_c[:, ns]
            cb = _dot_nt(c_g, b_g)
            st = state_ref[:, gs]
            st_b = st.astype(BF16)
            x_g = x_b[:, gs]
            c_g32 = c_g.astype(F32)
            y_g = x_c[:, gs] * dskip[:, gs]
            for j in range(SSD_HEADS_PER_GROUP):
                hd = g * SSD_HEADS_PER_GROUP + j
                seg = acs[:, hd:hd + 1] - acs_t[hd:hd + 1, :]
                decay = jnp.exp(jnp.where(causal, seg, -jnp.inf))
                scores = (cb * decay * dtt[hd:hd + 1, :]).astype(BF16)
                c_scaled = (c_g32 * eacs[:, hd:hd + 1]).astype(BF16)
                sel = lane_head == j
                lhs = jnp.concatenate([scores, c_scaled], axis=1)
                rhs = jnp.concatenate([jnp.where(sel, x_g, jnp.zeros_like(x_g)),
                                       jnp.where(sel, st_b, jnp.zeros_like(st_b))], axis=0)
                y_g = y_g + _dot(lhs, rhs)
            ys_ref[rows, gs] = y_g
            state_ref[:, gs] = st * elast_x[:, gs] + _dot_tn(b_g, xdd[:, gs])

    z = _dot(u, wz_ref[...])
    y = _rms(ys_ref[...] * _silu(z), gn_ref[...]).astype(BF16)
    o_ref[...] = h + _dot(y, wo_ref[...])


def _ssd(h, ng, in_proj, conv_w, conv_b, dt_bias, a_log, d_skip, norm_g, out_proj, *, tl):
    di, bc, nh = SSD_D_INNER, SSD_BC_DIM, SSD_N_HEADS
    o_x, o_b, o_c, o_dt = di, 2 * di, 2 * di + bc, 2 * di + 2 * bc

    def pad_heads(v):
        return jnp.pad(v, (0, LANES - nh))

    w_dt = jnp.pad(in_proj[:, o_dt:], ((0, 0), (0, LANES - nh)))
    head_of_col = jnp.arange(di, dtype=jnp.int32) // SSD_HEAD_DIM
    expand = (jnp.arange(LANES, dtype=jnp.int32)[:, None] == head_of_col[None, :]).astype(BF16)
    ii = jnp.arange(SSD_CHUNK, dtype=jnp.int32)
    tril = (ii[:, None] >= ii[None, :]).astype(BF16)
    weights = [
        ng.reshape(1, -1),
        in_proj[:, :o_x].astype(BF16), in_proj[:, o_x:o_b].astype(BF16),
        in_proj[:, o_b:o_c].astype(BF16), in_proj[:, o_c:o_dt].astype(BF16),
        w_dt.astype(BF16), w_dt.T.astype(BF16),
        conv_w[:, :di], conv_w[:, di:di + bc], conv_w[:, di + bc:],
        conv_b[None, :di], conv_b[None, di:di + bc], conv_b[None, di + bc:],
        pad_heads(dt_bias)[None, :], pad_heads(dt_bias)[:, None],
        pad_heads(a_log)[None, :], pad_heads(a_log)[:, None],
        jnp.repeat(d_skip, SSD_HEAD_DIM)[None, :],
        norm_g.reshape(1, -1), out_proj.astype(BF16),
        expand, tril, tril.T,
    ]
    pad = CONV_PAD
    scratch = [
        pltpu.VMEM((pad, di), F32), pltpu.VMEM((pad, bc), F32), pltpu.VMEM((pad, bc), F32),
        pltpu.VMEM((tl, di), F32), pltpu.VMEM((tl, bc), BF16), pltpu.VMEM((tl, bc), BF16),
        pltpu.VMEM((tl, di), F32), pltpu.VMEM((SSD_D_STATE, di), F32),
    ]
    return _call(functools.partial(_ssd_body, tl=tl), h, weights, scratch, tl, "ssd_mixer")


def _conf_body(h_ref, ng_ref, w1a_ref, w1b_ref, b1a_ref, b1b_ref, dww_ref, dwb_ref, lng_ref, lnb_ref,
               w2_ref, b2_ref, o_ref, *scratch, tl):
    pad, taps, rb = CONF_PAD, CONF_KERNEL, CONF_ROW_BLOCK
    nslab = D_MODEL // LANES
    slabs, (u_ref, cv_ref) = scratch[:nslab], scratch[nslab:]

    @pl.when(pl.program_id(1) == 0)
    def _():
        for s_ref in slabs:
            s_ref[0:pad, :] = jnp.zeros((pad, LANES), F32)

    u_ref[...] = _rms(h_ref[...], ng_ref[...]).astype(BF16)
    nch = w1a_ref.shape[0]
    per = nslab // nch
    for c in range(nch):
        glu = ((_dot(u_ref[...], w1a_ref[c]) + b1a_ref[c])
               * _sigmoid(_dot(u_ref[...], w1b_ref[c]) + b1b_ref[c]))
        for i in range(per):
            s_ref = slabs[c * per + i]
            cs = slice((c * per + i) * LANES, (c * per + i + 1) * LANES)
            s_ref[pad:pad + tl, :] = glu[:, i * LANES:(i + 1) * LANES]
            w_taps = [jnp.broadcast_to(dww_ref[k:k + 1, cs], (rb, LANES)) for k in range(taps)]
            bias = jnp.broadcast_to(dwb_ref[:, cs], (rb, LANES))
            for r0 in range(0, tl, rb):
                acc = bias
                for k in range(taps):
                    off = pad - (taps - 1) + k + r0
                    acc = acc + s_ref[off:off + rb, :] * w_taps[k]
                cv_ref[r0:r0 + rb, cs] = acc
            s_ref[0:pad, :] = s_ref[tl:tl + pad, :]

    v = _silu(_layer_norm(cv_ref[...], lng_ref[...], lnb_ref[...])).astype(BF16)
    o_ref[...] = h_ref[...] + _dot(v, w2_ref[...]) + b2_ref[...]


def _conformer(h, ng, pw1_w, pw1_b, dw_w, dw_b, ln_g, ln_b, pw2_w, pw2_b, *, tl):
    d, cc = D_MODEL, CONF_CHUNK
    nch = d // cc

    def cols(w):
        return jnp.transpose(w.reshape(w.shape[0], nch, cc), (1, 0, 2))

    weights = [
        ng.reshape(1, -1),
        cols(pw1_w[:, :d]).astype(BF16), cols(pw1_w[:, d:]).astype(BF16),
        cols(pw1_b[None, :d]), cols(pw1_b[None, d:]),
        dw_w, dw_b[None, :], ln_g[None, :], ln_b[None, :],
        pw2_w.astype(BF16), pw2_b[None, :],
    ]
    scratch = ([pltpu.VMEM((tl + CONF_PAD, LANES), F32)] * (d // LANES)
               + [pltpu.VMEM((tl, d), BF16), pltpu.VMEM((tl, d), F32)])
    return _call(functools.partial(_conf_body, tl=tl), h, weights, scratch, tl, "conformer_conv")


def _lru_body(h_ref, ng_ref, wg_ref, wx_ref, bg_ref, bx_ref, cw_ref, cb_ref, gaw_ref, gab_ref,
              gxw_ref, gxb_ref, lam_ref, wo_ref, bo_ref, o_ref,
              tail_ref, hprev_ref, sa_ref, sb_ref, *, tl):
    spad = tl // 2

    @pl.when(pl.program_id(1) == 0)
    def _():
        tail_ref[...] = jnp.zeros_like(tail_ref)
        hprev_ref[...] = jnp.zeros_like(hprev_ref)
        sa_ref[0:spad, :] = jnp.ones((spad, LRU_WIDTH), F32)
        sb_ref[0:spad, :] = jnp.zeros((spad, LRU_WIDTH), F32)

    h = h_ref[...]
    u = _rms(h, ng_ref[...]).astype(BF16)
    gate = _dot(u, wg_ref[...]) + bg_ref[...]
    xr = _causal_conv(_dot(u, wx_ref[...]) + bx_ref[...], tail_ref,
                      cw_ref[...], cb_ref[...], LRU_CONV, tl)
    r_parts, i_parts = [], []
    for k in range(LRU_N_BLOCKS):
        xb = xr[:, k * LRU_BLOCK:(k + 1) * LRU_BLOCK].astype(BF16)
        r_parts.append(_sigmoid(_dot(xb, gaw_ref[k]) + gab_ref[k]))
        i_parts.append(_sigmoid(_dot(xb, gxw_ref[k]) + gxb_ref[k]))
    r = jnp.concatenate(r_parts, axis=1)
    i = jnp.concatenate(i_parts, axis=1)
    log_a = (-LRU_C) * r * _softplus(-lam_ref[...])
    a = jnp.exp(log_a)
    b = jnp.sqrt(jnp.tanh(-log_a) * (1.0 + a * a)) * (i * xr)

    row = lax.broadcasted_iota(jnp.int32, (tl, 1), 0)
    b = b + jnp.where(row == 0, a * hprev_ref[...], 0.0)
    s = 1
    while s < tl:
        sa_ref[spad:spad + tl, :] = a
        sb_ref[spad:spad + tl, :] = b
        b = a * sb_ref[spad - s:spad - s + tl, :] + b
        if 2 * s < tl:
            a = a * sa_ref[spad - s:spad - s + tl, :]
        s *= 2
    hprev_ref[...] = b[tl - 1:tl, :]

    y = (_gelu_tanh(gate) * b).astype(BF16)
    o_ref[...] = h + _dot(y, wo_ref[...]) + bo_ref[...]


def _rglru(h, ng, in_w, in_b, conv_w, conv_b, ga_w, ga_b, gx_w, gx_b, lam, out_w, out_b, *, tl):
    w = LRU_WIDTH
    weights = [
        ng.reshape(1, -1),
        in_w[:, :w].astype(BF16), in_w[:, w:].astype(BF16), in_b[None, :w], in_b[None, w:],
        conv_w, conv_b[None, :],
        ga_w.astype(BF16), ga_b[:, None, :], gx_w.astype(BF16), gx_b[:, None, :],
        lam[None, :], out_w.astype(BF16), out_b[None, :],
    ]
    scratch = [
        pltpu.VMEM((CONV_PAD, w), F32), pltpu.VMEM((1, w), F32),
        pltpu.VMEM((tl + tl // 2, w), F32), pltpu.VMEM((tl + tl // 2, w), F32),
    ]
    return _call(functools.partial(_lru_body, tl=tl), h, weights, scratch, tl, "rglru_block")


def _sgu_body(h_ref, ng_ref, wu_ref, wv_ref, bu_ref, bv_ref, lng_ref, lnb_ref, spw_ref, spb_ref,
              wo_ref, bo_ref, o_ref, zu_ref, vn_ref, gt_ref, *, tl):
    q = SGU_CHUNK
    h = h_ref[...]
    u = _rms(h, ng_ref[...]).astype(BF16)
    zu_ref[...] = _gelu_tanh(_dot(u, wu_ref[...]) + bu_ref[...])
    v = _gelu_tanh(_dot(u, wv_ref[...]) + bv_ref[...])
    vn_ref[...] = _layer_norm(v, lng_ref[...], lnb_ref[...]).astype(BF16)

    causal = (lax.broadcasted_iota(jnp.int32, (q, q), 0) >= lax.broadcasted_iota(jnp.int32, (q, q), 1))
    spb = spb_ref[...]
    for g in range(SGU_GROUPS):
        gs = slice(g * SGU_GROUP_DIM, (g + 1) * SGU_GROUP_DIM)
        w = jnp.where(causal, spw_ref[g], 0.0).astype(BF16)
        bias = spb[:, g:g + 1]
        for c in range(tl // q):
            rows = slice(c * q, (c + 1) * q)
            mixed = _dot(w, vn_ref[rows, gs]) + bias
            gt_ref[rows, gs] = (zu_ref[rows, gs] * mixed).astype(BF16)
    o_ref[...] = h + _dot(gt_ref[...], wo_ref[...]) + bo_ref[...]


def _sgu(h, ng, in_w, in_b, ln_g, ln_b, sp_w, sp_b, out_w, out_b, *, tl):
    hf = SGU_HALF
    weights = [
        ng.reshape(1, -1),
        in_w[:, :hf].astype(BF16), in_w[:, hf:].astype(BF16), in_b[None, :hf], in_b[None, hf:],
        ln_g[None, :], ln_b[None, :], sp_w, sp_b.T,
        out_w.astype(BF16), out_b[None, :],
    ]
    scratch = [pltpu.VMEM((tl, hf), F32), pltpu.VMEM((tl, hf), BF16), pltpu.VMEM((tl, hf), BF16)]
    return _call(functools.partial(_sgu_body, tl=tl), h, weights, scratch, tl, "chunked_sgu")


def _tile(seqlen, want):
    return min(want, seqlen)


TL_SSD = 256
TL_CONF = 512
TL_LRU = 256
TL_SGU = 512
TL_FFN = 512


def kernel(x, norm_mix, norm_ffn, norm_final, a_in_proj, a_conv_w, a_conv_b, a_dt_bias, a_log, a_d_skip, a_norm, a_out_proj, b_pw1_w, b_pw1_b, b_dw_w, b_dw_b, b_ln_g, b_ln_b, b_pw2_w, b_pw2_b, c_in_w, c_in_b, c_conv_w, c_conv_b, c_ga_w, c_ga_b, c_gx_w, c_gx_b, c_lambda, c_out_w, c_out_b, d_in_w, d_in_b, d_ln_g, d_ln_b, d_sp_w, d_sp_b, d_out_w, d_out_b, f_up_w, f_conv_w, f_conv_b, f_down_w):
    depth = norm_mix.shape[0]
    seqlen = x.shape[1]
    h = x
    for i in range(depth):
        kind, j = i % 4, i // 4
        if kind == 0:
            h = _ssd(h, norm_mix[i], a_in_proj[j], a_conv_w[j], a_conv_b[j], a_dt_bias[j], a_log[j],
                     a_d_skip[j], a_norm[j], a_out_proj[j], tl=_tile(seqlen, TL_SSD))
        elif kind == 1:
            h = _conformer(h, norm_mix[i], b_pw1_w[j], b_pw1_b[j], b_dw_w[j], b_dw_b[j], b_ln_g[j],
                           b_ln_b[j], b_pw2_w[j], b_pw2_b[j], tl=_tile(seqlen, TL_CONF))
        elif kind == 2:
            h = _rglru(h, norm_mix[i], c_in_w[j], c_in_b[j], c_conv_w[j], c_conv_b[j], c_ga_w[j],
                       c_ga_b[j], c_gx_w[j], c_gx_b[j], c_lambda[j], c_out_w[j], c_out_b[j],
                       tl=_tile(seqlen, TL_LRU))
        else:
            h = _sgu(h, norm_mix[i], d_in_w[j], d_in_b[j], d_ln_g[j], d_ln_b[j], d_sp_w[j], d_sp_b[j],
                     d_out_w[j], d_out_b[j], tl=_tile(seqlen, TL_SGU))
        h = _ffn(h, norm_ffn[i], f_up_w[i], f_conv_w[i], f_conv_b[i], f_down_w[i], norm_final,
                 final_norm=(i == depth - 1), tl=_tile(seqlen, TL_FFN))
    return h
```

```python
import functools
import math

import jax
import jax.numpy as jnp
from jax import lax
from jax.experimental import pallas as pl
from jax.experimental.pallas import tpu as pltpu

F32 = jnp.float32
BF16 = jnp.bfloat16

RMS_EPS = 1e-6
LN_EPS = 1e-5
LOG2_E = math.log2(math.e)

D_MODEL = 1024

LANES = 128
SUBLANES = 8
VMEM_LIMIT_BYTES = 56 * 1024 * 1024
SSD_D_INNER = 2048
SSD_HEAD_DIM = 64
SSD_N_HEADS = 32
SSD_N_GROUPS = 8
SSD_HEADS_PER_GROUP = 4
SSD_D_STATE = 128
SSD_CONV = 4
SSD_CHUNK = 128
SSD_BC_DIM = 1024
SSD_GROUP_W = SSD_HEADS_PER_GROUP * SSD_HEAD_DIM

CONF_KERNEL = 31
CONF_PAD = 32
CONF_CHUNK = 256
CONF_ROW_BLOCK = 32

LRU_WIDTH = 1280
LRU_BLOCK = 256
LRU_N_BLOCKS = 5
LRU_CONV = 4
LRU_C = 8.0

SGU_CHUNK = 128
SGU_HALF = 2048
SGU_GROUPS = 8
SGU_GROUP_DIM = 256

FFN_HIDDEN = 2816
FFN_CONV = 3
FFN_CHUNK = 256
FFN_ROW_BLOCK = 32

CONV_PAD = SUBLANES


def _dot(a, b):
    return jnp.dot(a, b, preferred_element_type=F32)


def _dot_nt(a, b):
    return lax.dot_general(a, b, (((1,), (1,)), ((), ())), preferred_element_type=F32)


def _dot_tn(a, b):
    return lax.dot_general(a, b, (((0,), (0,)), ((), ())), preferred_element_type=F32)


def _rms(x, g):
    return x * lax.rsqrt(jnp.mean(x * x, axis=-1, keepdims=True) + RMS_EPS) * g


def _layer_norm(x, g, b):
    mu = jnp.mean(x, axis=-1, keepdims=True)
    xc = x - mu
    return xc * lax.rsqrt(jnp.mean(xc * xc, axis=-1, keepdims=True) + LN_EPS) * g + b


def _sigmoid(x):
    return 1.0 / (1.0 + jnp.exp2(x * (-LOG2_E)))


def _silu(x):
    return x * _sigmoid(x)


def _softplus(x):
    return jnp.maximum(x, 0.0) + jnp.log1p(jnp.exp(-jnp.abs(x)))


def _gelu_tanh(x):
    c = math.sqrt(2.0 / math.pi)
    return x * (0.5 * (1.0 + jnp.tanh(c * (x + 0.044715 * (x * x * x)))))


def _split_hi_lo(x):
    hi = x.astype(BF16)
    lo = (x - hi.astype(F32)).astype(BF16)
    return hi, lo


def _shift_rows(y, tail, s):
    rolled = pltpu.roll(y, s, axis=0)
    row = lax.broadcasted_iota(jnp.int32, (SUBLANES, 1), 0)
    head = jnp.where(row < s, pltpu.roll(tail, s, axis=0), rolled[0:SUBLANES, :])
    return jnp.concatenate([head, rolled[SUBLANES:, :]], axis=0)


def _causal_conv(y, tail_ref, w, b, taps, tl):
    tail = tail_ref[...]
    tail_ref[...] = y[tl - CONV_PAD:tl, :]
    acc = y * w[taps - 1:taps, :] + b
    for k in range(taps - 1):
        acc = acc + _shift_rows(y, tail, taps - 1 - k) * w[k:k + 1, :]
    return acc


def _wspec(shape):
    nd = len(shape)
    return pl.BlockSpec(shape, lambda b, l: (0,) * nd, pipeline_mode=pl.Buffered(1))


def _tile_spec(tl):
    return pl.BlockSpec((None, tl, D_MODEL), lambda b, l: (b, l, 0))


def _call(body, x, weights, scratch, tl, name):
    bsz, seqlen, _ = x.shape
    assert seqlen % tl == 0, (seqlen, tl)
    return pl.pallas_call(
        body,
        grid=(bsz, seqlen // tl),
        in_specs=[_tile_spec(tl)] + [_wspec(w.shape) for w in weights],
        out_specs=_tile_spec(tl),
        out_shape=jax.ShapeDtypeStruct(x.shape, x.dtype),
        scratch_shapes=scratch,
        compiler_params=pltpu.CompilerParams(
            dimension_semantics=("arbitrary", "arbitrary"),
            vmem_limit_bytes=VMEM_LIMIT_BYTES),
        name=name,
    )(x, *weights)


def _ffn_body(h_ref, ng_ref, up_ref, cw_ref, cb_ref, dn_ref, fg_ref,
              o_ref, tail_ref, yg0_ref, yv0_ref, yg1_ref, yv1_ref, u_ref, a_ref, *, tl, final_norm):
    pad, taps, rb, hid, hc = CONV_PAD, FFN_CONV, FFN_ROW_BLOCK, FFN_HIDDEN, FFN_CHUNK
    nch, nhalf = hid // hc, hc // LANES

    @pl.when(pl.program_id(1) == 0)
    def _():
        tail_ref[...] = jnp.zeros_like(tail_ref)

    u_ref[...] = _rms(h_ref[...], ng_ref[...]).astype(BF16)
    even_bufs, odd_bufs = (yg0_ref, yv0_ref), (yg1_ref, yv1_ref)

    def up_chunk(j, bufs):
        for part in range(2):
            c0 = part * hid + j * hc
            y = _dot(u_ref[...], up_ref[:, c0:c0 + hc])
            for hf in range(nhalf):
                cs = slice(c0 + hf * LANES, c0 + (hf + 1) * LANES)
                bufs[part][hf, 0:pad, :] = tail_ref[:, cs]
                bufs[part][hf, pad:pad + tl, :] = y[:, hf * LANES:(hf + 1) * LANES]
                tail_ref[:, cs] = bufs[part][hf, tl:tl + pad, :]

    def act_chunk(j, bufs):
        for hf in range(nhalf):
            for r0 in range(0, tl, rb):
                conv = []
                for part in range(2):
                    c0 = part * hid + j * hc + hf * LANES
                    acc = cb_ref[:, c0:c0 + LANES]
                    for k in range(taps):
                        off = pad - (taps - 1) + k + r0
                        acc = acc + bufs[part][hf, off:off + rb, :] * cw_ref[k:k + 1, c0:c0 + LANES]
                    conv.append(acc)
                a_ref[r0:r0 + rb, j * hc + hf * LANES:j * hc + (hf + 1) * LANES] = (
                    _silu(conv[0]) * conv[1]).astype(BF16)

    for j in range(nch):
        bufs = (even_bufs, odd_bufs)[j % 2]
        up_chunk(j, bufs)
        act_chunk(j, bufs)

    out = h_ref[...] + _dot(a_ref[...], dn_ref[...])
    if final_norm:
        out = _rms(out, fg_ref[...])
    o_ref[...] = out


def _ffn(h, ng, up_w, conv_w, conv_b, down_w, final_g, *, final_norm, tl):
    hid, hc = FFN_HIDDEN, FFN_CHUNK
    weights = [ng.reshape(1, -1), up_w.astype(BF16), conv_w, conv_b[None, :], down_w.astype(BF16),
               final_g.reshape(1, -1)]
    scratch = ([pltpu.VMEM((CONV_PAD, 2 * hid), F32)]
               + [pltpu.VMEM((hc // LANES, tl + CONV_PAD, LANES), F32)] * 4
               + [pltpu.VMEM((tl, D_MODEL), BF16), pltpu.VMEM((tl, hid), BF16)])
    body = functools.partial(_ffn_body, tl=tl, final_norm=final_norm)
    return _call(body, h, weights, scratch, tl, "conv_ffn")


def _ssd_body(h_ref, ng_ref, wz_ref, wx_ref, wb_ref, wc_ref, wdt_ref, wdtt_ref,
              cwx_ref, cwb_ref, cwc_ref, cbx_ref, cbb_ref, cbc_ref,
              dtb_r_ref, dtb_c_ref, alog_r_ref, alog_c_ref, dskip_ref, gn_ref, wo_ref,
              expand_ref, tril_ref, triu_ref,
              o_ref, *scratch, tl):
    q, pad, taps = SSD_CHUNK, CONV_PAD, SSD_CONV
    nx, nbc = SSD_D_INNER // LANES, SSD_BC_DIM // LANES
    xslabs, bslabs, cslabs = scratch[:nx], scratch[nx:nx + nbc], scratch[nx + nbc:nx + 2 * nbc]
    u_ref, xs_ref, bs_ref, cs_ref, ys_ref, state_ref = scratch[nx + 2 * nbc:]

    @pl.when(pl.program_id(1) == 0)
    def _():
        for slab in scratch[:nx + 2 * nbc]:
            slab[0:pad, :] = jnp.zeros((pad, LANES), F32)
        state_ref[...] = jnp.zeros_like(state_ref)

    h = h_ref[...]
    u_ref[...] = _rms(h, ng_ref[...]).astype(BF16)
    u = u_ref[...]

    def conv_silu(w_ref, cw_ref, cb_ref, slabs, out_ref):
        step = 2 * LANES
        for c0 in range(0, w_ref.shape[1], step):
            y = _dot(u_ref[...], w_ref[:, c0:c0 + step])
            for i in range(step // LANES):
                s = c0 // LANES + i
                cs = slice(s * LANES, (s + 1) * LANES)
                slabs[s][pad:pad + tl, :] = y[:, i * LANES:(i + 1) * LANES]
                acc = cb_ref[:, cs]
                for t in range(taps):
                    off = pad - (taps - 1) + t
                    acc = acc + slabs[s][off:off + tl, :] * cw_ref[t:t + 1, cs]
                slabs[s][0:pad, :] = slabs[s][tl:tl + pad, :]
                out_ref[:, cs] = _silu(acc).astype(out_ref.dtype)

    conv_silu(wx_ref, cwx_ref, cbx_ref, xslabs, xs_ref)
    conv_silu(wb_ref, cwb_ref, cbb_ref, bslabs, bs_ref)
    conv_silu(wc_ref, cwc_ref, cbc_ref, cslabs, cs_ref)

    head_r = lax.broadcasted_iota(jnp.int32, (1, LANES), 1) < SSD_N_HEADS
    head_c = lax.broadcasted_iota(jnp.int32, (LANES, 1), 0) < SSD_N_HEADS
    a_r = jnp.where(head_r, -jnp.exp(alog_r_ref[...]), 0.0)
    a_c = jnp.where(head_c, -jnp.exp(alog_c_ref[...]), 0.0)
    dt_all = _softplus(_dot(u, wdt_ref[...]) + dtb_r_ref[...])
    dtt_all = _softplus(_dot_nt(wdtt_ref[...], u) + dtb_c_ref[...])

    causal = (lax.broadcasted_iota(jnp.int32, (q, q), 0) >= lax.broadcasted_iota(jnp.int32, (q, q), 1))
    lane_head = lax.broadcasted_iota(jnp.int32, (1, SSD_GROUP_W), 1) // SSD_HEAD_DIM
    row16 = lax.broadcasted_iota(jnp.int32, (16, LANES), 0)
    expand = expand_ref[...]
    tril = tril_ref[...]
    triu = triu_ref[...]
    dskip = dskip_ref[...]

    for c in range(tl // q):
        rows = slice(c * q, (c + 1) * q)
        dt = dt_all[rows, :]
        dtt = dtt_all[:, rows]
        hi, lo = _split_hi_lo(dt * a_r)
        acs = _dot(tril, hi) + _dot(tril, lo)
        hi_t, lo_t = _split_hi_lo(dtt * a_c)
        acs_t = _dot(hi_t, triu) + _dot(lo_t, triu)
        eacs = jnp.exp(acs)
        acs_last = acs[q - 1:q, :]
        dend = jnp.exp(acs_last - acs) * dt
        el_hi, el_lo = _split_hi_lo(jnp.exp(acs_last))
        el = jnp.where(row16 == 0, el_hi.astype(F32), jnp.where(row16 == 1, el_lo.astype(F32), 0.0))
        el_x = _dot(el.astype(BF16), expand)
        elast_x = el_x[0:1, :] + el_x[1:2, :]
        dend_x = _dot(dend.astype(BF16), expand)

        x_c = xs_ref[rows, :]
        x_b = x_c.astype(BF16)
        xdd = (x_c * dend_x).astype(BF16)
        b_c = bs_ref[rows, :]
        c_c = cs_ref[rows, :]
        for g in range(SSD_N_GROUPS):
            gs = slice(g * SSD_GROUP_W, (g + 1) * SSD_GROUP_W)
            ns = slice(g * SSD_D_STATE, (g + 1) * SSD_D_STATE)
            b_g = b_c[:, ns]
            c_g = c_c[:, ns]
            cb = _dot_nt(c_g, b_g)
            st = state_ref[:, gs]
            st_b = st.astype(BF16)
            x_g = x_b[:, gs]
            c_g32 = c_g.astype(F32)
            y_g = x_c[:, gs] * dskip[:, gs]
            for j in range(SSD_HEADS_PER_GROUP):
                hd = g * SSD_HEADS_PER_GROUP + j
                seg = acs[:, hd:hd + 1] - acs_t[hd:hd + 1, :]
                decay = jnp.exp(jnp.where(causal, seg, -jnp.inf))
                scores = (cb * decay * dtt[hd:hd + 1, :]).astype(BF16)
                c_scaled = (c_g32 * eacs[:, hd:hd + 1]).astype(BF16)
                sel = lane_head == j
                lhs = jnp.concatenate([scores, c_scaled], axis=1)
                rhs = jnp.concatenate([jnp.where(sel, x_g, jnp.zeros_like(x_g)),
                                       jnp.where(sel, st_b, jnp.zeros_like(st_b))], axis=0)
                y_g = y_g + _dot(lhs, rhs)
            ys_ref[rows, gs] = y_g
            state_ref[:, gs] = st * elast_x[:, gs] + _dot_tn(b_g, xdd[:, gs])

    z = _dot(u, wz_ref[...])
    y = _rms(ys_ref[...] * _silu(z), gn_ref[...]).astype(BF16)
    o_ref[...] = h + _dot(y, wo_ref[...])


def _ssd(h, ng, in_proj, conv_w, conv_b, dt_bias, a_log, d_skip, norm_g, out_proj, *, tl):
    di, bc, nh = SSD_D_INNER, SSD_BC_DIM, SSD_N_HEADS
    o_x, o_b, o_c, o_dt = di, 2 * di, 2 * di + bc, 2 * di + 2 * bc

    def pad_heads(v):
        return jnp.pad(v, (0, LANES - nh))

    w_dt = jnp.pad(in_proj[:, o_dt:], ((0, 0), (0, LANES - nh)))
    head_of_col = jnp.arange(di, dtype=jnp.int32) // SSD_HEAD_DIM
    expand = (jnp.arange(LANES, dtype=jnp.int32)[:, None] == head_of_col[None, :]).astype(BF16)
    ii = jnp.arange(SSD_CHUNK, dtype=jnp.int32)
    tril = (ii[:, None] >= ii[None, :]).astype(BF16)
    weights = [
        ng.reshape(1, -1),
        in_proj[:, :o_x].astype(BF16), in_proj[:, o_x:o_b].astype(BF16),
        in_proj[:, o_b:o_c].astype(BF16), in_proj[:, o_c:o_dt].astype(BF16),
        w_dt.astype(BF16), w_dt.T.astype(BF16),
        conv_w[:, :di], conv_w[:, di:di + bc], conv_w[:, di + bc:],
        conv_b[None, :di], conv_b[None, di:di + bc], conv_b[None, di + bc:],
        pad_heads(dt_bias)[None, :], pad_heads(dt_bias)[:, None],
        pad_heads(a_log)[None, :], pad_heads(a_log)[:, None],
        jnp.repeat(d_skip, SSD_HEAD_DIM)[None, :],
        norm_g.reshape(1, -1), out_proj.astype(BF16),
        expand, tril, tril.T,
    ]
    scratch = ([pltpu.VMEM((tl + CONV_PAD, LANES), F32)] * ((di + 2 * bc) // LANES) + [
        pltpu.VMEM((tl, D_MODEL), BF16),
        pltpu.VMEM((tl, di), F32), pltpu.VMEM((tl, bc), BF16), pltpu.VMEM((tl, bc), BF16),
        pltpu.VMEM((tl, di), F32), pltpu.VMEM((SSD_D_STATE, di), F32),
    ])
    return _call(functools.partial(_ssd_body, tl=tl), h, weights, scratch, tl, "ssd_mixer")


def _conf_body(h_ref, ng_ref, w1_ref, b1_ref, dww_ref, dwb_ref, lng_ref, lnb_ref,
               w2_ref, b2_ref, o_ref, *scratch, tl):
    pad, taps, rb, d, cc = CONF_PAD, CONF_KERNEL, CONF_ROW_BLOCK, D_MODEL, CONF_CHUNK
    nslab = d // LANES
    slabs, (u_ref, cv_ref) = scratch[:nslab], scratch[nslab:]

    @pl.when(pl.program_id(1) == 0)
    def _():
        for s_ref in slabs:
            s_ref[0:pad, :] = jnp.zeros((pad, LANES), F32)

    u_ref[...] = _rms(h_ref[...], ng_ref[...]).astype(BF16)
    per = cc // LANES
    for c in range(d // cc):
        ca, cg = slice(c * cc, (c + 1) * cc), slice(d + c * cc, d + (c + 1) * cc)
        glu = ((_dot(u_ref[...], w1_ref[:, ca]) + b1_ref[:, ca])
               * _sigmoid(_dot(u_ref[...], w1_ref[:, cg]) + b1_ref[:, cg]))
        for i in range(per):
            s_ref = slabs[c * per + i]
            cs = slice((c * per + i) * LANES, (c * per + i + 1) * LANES)
            s_ref[pad:pad + tl, :] = glu[:, i * LANES:(i + 1) * LANES]
            w_taps = [jnp.broadcast_to(dww_ref[k:k + 1, cs], (rb, LANES)) for k in range(taps)]
            bias = jnp.broadcast_to(dwb_ref[:, cs], (rb, LANES))
            for r0 in range(0, tl, rb):
                acc = bias
                for k in range(taps):
                    off = pad - (taps - 1) + k + r0
                    acc = acc + s_ref[off:off + rb, :] * w_taps[k]
                cv_ref[r0:r0 + rb, cs] = acc
            s_ref[0:pad, :] = s_ref[tl:tl + pad, :]

    v = _silu(_layer_norm(cv_ref[...], lng_ref[...], lnb_ref[...])).astype(BF16)
    o_ref[...] = h_ref[...] + _dot(v, w2_ref[...]) + b2_ref[...]


def _conformer(h, ng, pw1_w, pw1_b, dw_w, dw_b, ln_g, ln_b, pw2_w, pw2_b, *, tl):
    d = D_MODEL
    weights = [
        ng.reshape(1, -1), pw1_w.astype(BF16), pw1_b[None, :],
        dw_w, dw_b[None, :], ln_g[None, :], ln_b[None, :],
        pw2_w.astype(BF16), pw2_b[None, :],
    ]
    scratch = ([pltpu.VMEM((tl + CONF_PAD, LANES), F32)] * (d // LANES)
               + [pltpu.VMEM((tl, d), BF16), pltpu.VMEM((tl, d), F32)])
    return _call(functools.partial(_conf_body, tl=tl), h, weights, scratch, tl, "conformer_conv")


def _lru_body(h_ref, ng_ref, wg_ref, wx_ref, bg_ref, bx_ref, cw_ref, cb_ref, gaw_ref, gab_ref,
              gxw_ref, gxb_ref, lam_ref, wo_ref, bo_ref, o_ref, *scratch, tl):
    pad, taps, spad = CONV_PAD, LRU_CONV, tl // 2
    nslab = LRU_WIDTH // LANES
    per = LRU_BLOCK // LANES
    xslabs, sa, sb = scratch[:nslab], scratch[nslab:2 * nslab], scratch[2 * nslab:3 * nslab]
    u_ref, hprev_ref, y_ref = scratch[3 * nslab:]

    @pl.when(pl.program_id(1) == 0)
    def _():
        hprev_ref[...] = jnp.zeros_like(hprev_ref)
        for s in range(nslab):
            xslabs[s][0:pad, :] = jnp.zeros((pad, LANES), F32)
            sa[s][0:spad, :] = jnp.ones((spad, LANES), F32)
            sb[s][0:spad, :] = jnp.zeros((spad, LANES), F32)

    u_ref[...] = _rms(h_ref[...], ng_ref[...]).astype(BF16)
    row8 = lax.broadcasted_iota(jnp.int32, (SUBLANES, LANES), 0)
    for k in range(LRU_N_BLOCKS):
        bs = slice(k * LRU_BLOCK, (k + 1) * LRU_BLOCK)
        x_raw = _dot(u_ref[...], wx_ref[:, bs]) + bx_ref[:, bs]
        x_parts = []
        for i in range(per):
            s = k * per + i
            cs = slice(s * LANES, (s + 1) * LANES)
            xslabs[s][pad:pad + tl, :] = x_raw[:, i * LANES:(i + 1) * LANES]
            acc = cb_ref[:, cs]
            for t in range(taps):
                off = pad - (taps - 1) + t
                acc = acc + xslabs[s][off:off + tl, :] * cw_ref[t:t + 1, cs]
            xslabs[s][0:pad, :] = xslabs[s][tl:tl + pad, :]
            x_parts.append(acc)
        xr = jnp.concatenate(x_parts, axis=1)
        xb = xr.astype(BF16)
        r = _sigmoid(_dot(xb, gaw_ref[k]) + gab_ref[k])
        ig = _sigmoid(_dot(xb, gxw_ref[k]) + gxb_ref[k])
        log_a = (-LRU_C) * r * _softplus(-lam_ref[:, bs])
        a_blk = jnp.exp(log_a)
        b_blk = jnp.sqrt(jnp.tanh(-log_a) * (1.0 + a_blk * a_blk)) * (ig * xr)
        gate = _gelu_tanh(_dot(u_ref[...], wg_ref[:, bs]) + bg_ref[:, bs])

        for i in range(per):
            s = k * per + i
            cs = slice(s * LANES, (s + 1) * LANES)
            a = a_blk[:, i * LANES:(i + 1) * LANES]
            b = b_blk[:, i * LANES:(i + 1) * LANES]
            b_head = b[0:SUBLANES, :] + jnp.where(row8 == 0, a[0:SUBLANES, :] * hprev_ref[:, cs], 0.0)
            b = jnp.concatenate([b_head, b[SUBLANES:, :]], axis=0)
            d = 1
            while d < tl:
                sa[s][spad:spad + tl, :] = a
                sb[s][spad:spad + tl, :] = b
                b = a * sb[s][spad - d:spad - d + tl, :] + b
                if 2 * d < tl:
                    a = a * sa[s][spad - d:spad - d + tl, :]
                d *= 2
            hprev_ref[:, cs] = b[tl - 1:tl, :]
            y_ref[:, cs] = (gate[:, i * LANES:(i + 1) * LANES] * b).astype(BF16)

    o_ref[...] = h_ref[...] + _dot(y_ref[...], wo_ref[...]) + bo_ref[...]


def _rglru(h, ng, in_w, in_b, conv_w, conv_b, ga_w, ga_b, gx_w, gx_b, lam, out_w, out_b, *, tl):
    w = LRU_WIDTH
    weights = [
        ng.reshape(1, -1),
        in_w[:, :w].astype(BF16), in_w[:, w:].astype(BF16), in_b[None, :w], in_b[None, w:],
        conv_w, conv_b[None, :],
        ga_w.astype(BF16), ga_b[:, None, :], gx_w.astype(BF16), gx_b[:, None, :],
        lam[None, :], out_w.astype(BF16), out_b[None, :],
    ]
    nslab = w // LANES
    scratch = ([pltpu.VMEM((tl + CONV_PAD, LANES), F32)] * nslab
               + [pltpu.VMEM((tl + tl // 2, LANES), F32)] * (2 * nslab)
               + [pltpu.VMEM((tl, D_MODEL), BF16), pltpu.VMEM((1, w), F32), pltpu.VMEM((tl, w), BF16)])
    return _call(functools.partial(_lru_body, tl=tl), h, weights, scratch, tl, "rglru_block")


def _sgu_body(h_ref, ng_ref, wu_ref, wv_ref, bu_ref, bv_ref, lng_ref, lnb_ref, spw_ref, spb_ref,
              wo_ref, bo_ref, o_ref, zu_ref, vn_ref, gt_ref, *, tl):
    q = SGU_CHUNK
    h = h_ref[...]
    u = _rms(h, ng_ref[...]).astype(BF16)
    zu_ref[...] = _gelu_tanh(_dot(u, wu_ref[...]) + bu_ref[...])
    v = _gelu_tanh(_dot(u, wv_ref[...]) + bv_ref[...])
    vn_ref[...] = _layer_norm(v, lng_ref[...], lnb_ref[...]).astype(BF16)

    causal = (lax.broadcasted_iota(jnp.int32, (q, q), 0) >= lax.broadcasted_iota(jnp.int32, (q, q), 1))
    spb = spb_ref[...]
    for g in range(SGU_GROUPS):
        gs = slice(g * SGU_GROUP_DIM, (g + 1) * SGU_GROUP_DIM)
        w = jnp.where(causal, spw_ref[g], 0.0).astype(BF16)
        bias = spb[:, g:g + 1]
        for c in range(tl // q):
            rows = slice(c * q, (c + 1) * q)
            mixed = _dot(w, vn_ref[rows, gs]) + bias
            gt_ref[rows, gs] = (zu_ref[rows, gs] * mixed).astype(BF16)
    o_ref[...] = h + _dot(gt_ref[...], wo_ref[...]) + bo_ref[...]


def _sgu(h, ng, in_w, in_b, ln_g, ln_b, sp_w, sp_b, out_w, out_b, *, tl):
    hf = SGU_HALF
    weights = [
        ng.reshape(1, -1),
        in_w[:, :hf].astype(BF16), in_w[:, hf:].astype(BF16), in_b[None, :hf], in_b[None, hf:],
        ln_g[None, :], ln_b[None, :], sp_w, sp_b.T,
        out_w.astype(BF16), out_b[None, :],
    ]
    scratch = [pltpu.VMEM((tl, hf), F32), pltpu.VMEM((tl, hf), BF16), pltpu.VMEM((tl, hf), BF16)]
    return _call(functools.partial(_sgu_body, tl=tl), h, weights, scratch, tl, "chunked_sgu")


def _tile(seqlen, want):
    return min(want, seqlen)


TL_SSD = 256
TL_CONF = 512
TL_LRU = 256
TL_SGU = 512
TL_FFN = 512


def kernel(x, norm_mix, norm_ffn, norm_final, a_in_proj, a_conv_w, a_conv_b, a_dt_bias, a_log, a_d_skip, a_norm, a_out_proj, b_pw1_w, b_pw1_b, b_dw_w, b_dw_b, b_ln_g, b_ln_b, b_pw2_w, b_pw2_b, c_in_w, c_in_b, c_conv_w, c_conv_b, c_ga_w, c_ga_b, c_gx_w, c_gx_b, c_lambda, c_out_w, c_out_b, d_in_w, d_in_b, d_ln_g, d_ln_b, d_sp_w, d_sp_b, d_out_w, d_out_b, f_up_w, f_conv_w, f_conv_b, f_down_w):
    depth = norm_mix.shape[0]
    seqlen = x.shape[1]
    h = x
    for i in range(depth):
        kind, j = i % 4, i // 4
        if kind == 0:
            h = _ssd(h, norm_mix[i], a_in_proj[j], a_conv_w[j], a_conv_b[j], a_dt_bias[j], a_log[j],
                     a_d_skip[j], a_norm[j], a_out_proj[j], tl=_tile(seqlen, TL_SSD))
        elif kind == 1:
            h = _conformer(h, norm_mix[i], b_pw1_w[j], b_pw1_b[j], b_dw_w[j], b_dw_b[j], b_ln_g[j],
                           b_ln_b[j], b_pw2_w[j], b_pw2_b[j], tl=_tile(seqlen, TL_CONF))
        elif kind == 2:
            h = _rglru(h, norm_mix[i], c_in_w[j], c_in_b[j], c_conv_w[j], c_conv_b[j], c_ga_w[j],
                       c_ga_b[j], c_gx_w[j], c_gx_b[j], c_lambda[j], c_out_w[j], c_out_b[j],
                       tl=_tile(seqlen, TL_LRU))
        else:
            h = _sgu(h, norm_mix[i], d_in_w[j], d_in_b[j], d_ln_g[j], d_ln_b[j], d_sp_w[j], d_sp_b[j],
                     d_out_w[j], d_out_b[j], tl=_tile(seqlen, TL_SGU))
        h = _ffn(h, norm_ffn[i], f_up_w[i], f_conv_w[i], f_conv_b[i], f_down_w[i], norm_final,
                 final_norm=(i == depth - 1), tl=_tile(seqlen, TL_FFN))
    return h
```

```python
import functools
import math

import jax
import jax.numpy as jnp
from jax import lax
from jax.experimental import pallas as pl
from jax.experimental.pallas import tpu as pltpu

F32 = jnp.float32
BF16 = jnp.bfloat16

RMS_EPS = 1e-6
LN_EPS = 1e-5
LOG2_E = math.log2(math.e)

D_MODEL = 1024

LANES = 128
SUBLANES = 8
VMEM_LIMIT_BYTES = 56 * 1024 * 1024
SSD_D_INNER = 2048
SSD_HEAD_DIM = 64
SSD_N_HEADS = 32
SSD_N_GROUPS = 8
SSD_HEADS_PER_GROUP = 4
SSD_D_STATE = 128
SSD_CONV = 4
SSD_CHUNK = 128
SSD_BC_DIM = 1024
SSD_GROUP_W = SSD_HEADS_PER_GROUP * SSD_HEAD_DIM

CONF_KERNEL = 31
CONF_PAD = 32
CONF_CHUNK = 256
CONF_ROW_BLOCK = 64

LRU_WIDTH = 1280
LRU_BLOCK = 256
LRU_N_BLOCKS = 5
LRU_CONV = 4
LRU_C = 8.0

SGU_CHUNK = 128
SGU_HALF = 2048
SGU_GROUPS = 8
SGU_GROUP_DIM = 256

FFN_HIDDEN = 2816
FFN_CONV = 3
FFN_CHUNK = 256
FFN_ROW_BLOCK = 32

CONV_PAD = SUBLANES


def _dot(a, b):
    return jnp.dot(a, b, preferred_element_type=F32)


def _dot_nt(a, b):
    return lax.dot_general(a, b, (((1,), (1,)), ((), ())), preferred_element_type=F32)


def _dot_tn(a, b):
    return lax.dot_general(a, b, (((0,), (0,)), ((), ())), preferred_element_type=F32)


def _rms(x, g):
    return x * lax.rsqrt(jnp.mean(x * x, axis=-1, keepdims=True) + RMS_EPS) * g


def _layer_norm(x, g, b):
    mu = jnp.mean(x, axis=-1, keepdims=True)
    xc = x - mu
    return xc * lax.rsqrt(jnp.mean(xc * xc, axis=-1, keepdims=True) + LN_EPS) * g + b


def _sigmoid(x):
    return 1.0 / (1.0 + jnp.exp2(x * (-LOG2_E)))


def _silu(x):
    return x * _sigmoid(x)


def _softplus(x):
    return jnp.maximum(x, 0.0) + jnp.log1p(jnp.exp(-jnp.abs(x)))


def _gelu_tanh(x):
    c = math.sqrt(2.0 / math.pi)
    return x * (0.5 * (1.0 + jnp.tanh(c * (x + 0.044715 * (x * x * x)))))


def _split_hi_lo(x):
    hi = x.astype(BF16)
    lo = (x - hi.astype(F32)).astype(BF16)
    return hi, lo


def _shift_rows(y, tail, s):
    rolled = pltpu.roll(y, s, axis=0)
    row = lax.broadcasted_iota(jnp.int32, (SUBLANES, 1), 0)
    head = jnp.where(row < s, pltpu.roll(tail, s, axis=0), rolled[0:SUBLANES, :])
    return jnp.concatenate([head, rolled[SUBLANES:, :]], axis=0)


def _causal_conv(y, tail_ref, w, b, taps, tl):
    tail = tail_ref[...]
    tail_ref[...] = y[tl - CONV_PAD:tl, :]
    acc = y * w[taps - 1:taps, :] + b
    for k in range(taps - 1):
        acc = acc + _shift_rows(y, tail, taps - 1 - k) * w[k:k + 1, :]
    return acc


def _wspec(shape):
    nd = len(shape)
    return pl.BlockSpec(shape, lambda b, l: (0,) * nd, pipeline_mode=pl.Buffered(1))


def _tile_spec(tl):
    return pl.BlockSpec((None, tl, D_MODEL), lambda b, l: (b, l, 0))


def _call(body, x, weights, scratch, tl, name):
    bsz, seqlen, _ = x.shape
    assert seqlen % tl == 0, (seqlen, tl)
    return pl.pallas_call(
        body,
        grid=(bsz, seqlen // tl),
        in_specs=[_tile_spec(tl)] + [_wspec(w.shape) for w in weights],
        out_specs=_tile_spec(tl),
        out_shape=jax.ShapeDtypeStruct(x.shape, x.dtype),
        scratch_shapes=scratch,
        compiler_params=pltpu.CompilerParams(
            dimension_semantics=("arbitrary", "arbitrary"),
            vmem_limit_bytes=VMEM_LIMIT_BYTES),
        name=name,
    )(x, *weights)


def _ffn_body(h_ref, ng_ref, up_ref, cw_ref, cb_ref, dn_ref, fg_ref,
              o_ref, tail_ref, yg0_ref, yv0_ref, yg1_ref, yv1_ref, u_ref, a_ref, *, tl, final_norm):
    pad, taps, rb, hid, hc = CONV_PAD, FFN_CONV, FFN_ROW_BLOCK, FFN_HIDDEN, FFN_CHUNK
    nch, nhalf = hid // hc, hc // LANES

    @pl.when(pl.program_id(1) == 0)
    def _():
        tail_ref[...] = jnp.zeros_like(tail_ref)

    u_ref[...] = _rms(h_ref[...], ng_ref[...]).astype(BF16)
    even_bufs, odd_bufs = (yg0_ref, yv0_ref), (yg1_ref, yv1_ref)

    def up_chunk(j, bufs):
        for part in range(2):
            c0 = part * hid + j * hc
            y = _dot(u_ref[...], up_ref[:, c0:c0 + hc])
            for hf in range(nhalf):
                cs = slice(c0 + hf * LANES, c0 + (hf + 1) * LANES)
                bufs[part][hf, 0:pad, :] = tail_ref[:, cs]
                bufs[part][hf, pad:pad + tl, :] = y[:, hf * LANES:(hf + 1) * LANES]
                tail_ref[:, cs] = bufs[part][hf, tl:tl + pad, :]

    def act_chunk(j, bufs):
        for hf in range(nhalf):
            for r0 in range(0, tl, rb):
                conv = []
                for part in range(2):
                    c0 = part * hid + j * hc + hf * LANES
                    acc = cb_ref[:, c0:c0 + LANES]
                    for k in range(taps):
                        off = pad - (taps - 1) + k + r0
                        acc = acc + bufs[part][hf, off:off + rb, :] * cw_ref[k:k + 1, c0:c0 + LANES]
                    conv.append(acc)
                a_ref[r0:r0 + rb, j * hc + hf * LANES:j * hc + (hf + 1) * LANES] = (
                    _silu(conv[0]) * conv[1]).astype(BF16)

    for j in range(nch):
        bufs = (even_bufs, odd_bufs)[j % 2]
        up_chunk(j, bufs)
        act_chunk(j, bufs)

    out = h_ref[...] + _dot(a_ref[...], dn_ref[...])
    if final_norm:
        out = _rms(out, fg_ref[...])
    o_ref[...] = out


def _ffn(h, ng, up_w, conv_w, conv_b, down_w, final_g, *, final_norm, tl):
    hid, hc = FFN_HIDDEN, FFN_CHUNK
    weights = [ng.reshape(1, -1), up_w.astype(BF16), conv_w, conv_b[None, :], down_w.astype(BF16),
               final_g.reshape(1, -1)]
    scratch = ([pltpu.VMEM((CONV_PAD, 2 * hid), F32)]
               + [pltpu.VMEM((hc // LANES, tl + CONV_PAD, LANES), F32)] * 4
               + [pltpu.VMEM((tl, D_MODEL), BF16), pltpu.VMEM((tl, hid), BF16)])
    body = functools.partial(_ffn_body, tl=tl, final_norm=final_norm)
    return _call(body, h, weights, scratch, tl, "conv_ffn")


def _ssd_body(h_ref, ng_ref, wz_ref, wx_ref, wb_ref, wc_ref, wdt_ref, wdtt_ref,
              cwx_ref, cwb_ref, cwc_ref, cbx_ref, cbb_ref, cbc_ref,
              dtb_r_ref, dtb_c_ref, alog_r_ref, alog_c_ref, dskip_ref, gn_ref, wo_ref,
              expand_ref, tril_ref, triu_ref,
              o_ref, *scratch, tl):
    q, pad, taps = SSD_CHUNK, CONV_PAD, SSD_CONV
    nx, nbc = SSD_D_INNER // LANES, SSD_BC_DIM // LANES
    xslabs, bslabs, cslabs = scratch[:nx], scratch[nx:nx + nbc], scratch[nx + nbc:nx + 2 * nbc]
    u_ref, xs_ref, bs_ref, cs_ref, ys_ref, state_ref = scratch[nx + 2 * nbc:]

    @pl.when(pl.program_id(1) == 0)
    def _():
        for slab in scratch[:nx + 2 * nbc]:
            slab[0:pad, :] = jnp.zeros((pad, LANES), F32)
        state_ref[...] = jnp.zeros_like(state_ref)

    h = h_ref[...]
    u_ref[...] = _rms(h, ng_ref[...]).astype(BF16)
    u = u_ref[...]

    def conv_silu(w_ref, cw_ref, cb_ref, slabs, out_ref):
        step = 2 * LANES
        for c0 in range(0, w_ref.shape[1], step):
            y = _dot(u_ref[...], w_ref[:, c0:c0 + step])
            for i in range(step // LANES):
                s = c0 // LANES + i
                cs = slice(s * LANES, (s + 1) * LANES)
                slabs[s][pad:pad + tl, :] = y[:, i * LANES:(i + 1) * LANES]
                acc = cb_ref[:, cs]
                for t in range(taps):
                    off = pad - (taps - 1) + t
                    acc = acc + slabs[s][off:off + tl, :] * cw_ref[t:t + 1, cs]
                slabs[s][0:pad, :] = slabs[s][tl:tl + pad, :]
                out_ref[:, cs] = _silu(acc).astype(out_ref.dtype)

    conv_silu(wx_ref, cwx_ref, cbx_ref, xslabs, xs_ref)
    conv_silu(wb_ref, cwb_ref, cbb_ref, bslabs, bs_ref)
    conv_silu(wc_ref, cwc_ref, cbc_ref, cslabs, cs_ref)

    head_r = lax.broadcasted_iota(jnp.int32, (1, LANES), 1) < SSD_N_HEADS
    head_c = lax.broadcasted_iota(jnp.int32, (LANES, 1), 0) < SSD_N_HEADS
    a_r = jnp.where(head_r, -jnp.exp(alog_r_ref[...]), 0.0)
    a_c = jnp.where(head_c, -jnp.exp(alog_c_ref[...]), 0.0)
    dt_all = _softplus(_dot(u, wdt_ref[...]) + dtb_r_ref[...])
    dtt_all = _softplus(_dot_nt(wdtt_ref[...], u) + dtb_c_ref[...])

    causal = (lax.broadcasted_iota(jnp.int32, (q, q), 0) >= lax.broadcasted_iota(jnp.int32, (q, q), 1))
    lane_head = lax.broadcasted_iota(jnp.int32, (1, SSD_GROUP_W), 1) // SSD_HEAD_DIM
    row16 = lax.broadcasted_iota(jnp.int32, (16, LANES), 0)
    expand = expand_ref[...]
    tril = tril_ref[...]
    triu = triu_ref[...]
    dskip = dskip_ref[...]

    for c in range(tl // q):
        rows = slice(c * q, (c + 1) * q)
        dt = dt_all[rows, :]
        dtt = dtt_all[:, rows]
        hi, lo = _split_hi_lo(dt * a_r)
        acs = _dot(tril, hi) + _dot(tril, lo)
        hi_t, lo_t = _split_hi_lo(dtt * a_c)
        acs_t = _dot(hi_t, triu) + _dot(lo_t, triu)
        eacs = jnp.exp(acs)
        acs_last = acs[q - 1:q, :]
        dend = jnp.exp(acs_last - acs) * dt
        el_hi, el_lo = _split_hi_lo(jnp.exp(acs_last))
        el = jnp.where(row16 == 0, el_hi.astype(F32), jnp.where(row16 == 1, el_lo.astype(F32), 0.0))
        el_x = _dot(el.astype(BF16), expand)
        elast_x = el_x[0:1, :] + el_x[1:2, :]
        dend_x = _dot(dend.astype(BF16), expand)

        x_c = xs_ref[rows, :]
        x_b = x_c.astype(BF16)
        xdd = (x_c * dend_x).astype(BF16)
        b_c = bs_ref[rows, :]
        c_c = cs_ref[rows, :]
        for g in range(SSD_N_GROUPS):
            gs = slice(g * SSD_GROUP_W, (g + 1) * SSD_GROUP_W)
            ns = slice(g * SSD_D_STATE, (g + 1) * SSD_D_STATE)
            b_g = b_c[:, ns]
            c_g = c_c[:, ns]
            cb = _dot_nt(c_g, b_g)
            st = state_ref[:, gs]
            st_b = st.astype(BF16)
            x_g = x_b[:, gs]
            c_g32 = c_g.astype(F32)
            y_g = x_c[:, gs] * dskip[:, gs]
            for j in range(SSD_HEADS_PER_GROUP):
                hd = g * SSD_HEADS_PER_GROUP + j
                seg = acs[:, hd:hd + 1] - acs_t[hd:hd + 1, :]
                decay = jnp.exp(jnp.where(causal, seg, -jnp.inf))
                scores = (cb * decay * dtt[hd:hd + 1, :]).astype(BF16)
                c_scaled = (c_g32 * eacs[:, hd:hd + 1]).astype(BF16)
                sel = lane_head == j
                lhs = jnp.concatenate([scores, c_scaled], axis=1)
                rhs = jnp.concatenate([jnp.where(sel, x_g, jnp.zeros_like(x_g)),
                                       jnp.where(sel, st_b, jnp.zeros_like(st_b))], axis=0)
                y_g = y_g + _dot(lhs, rhs)
            ys_ref[rows, gs] = y_g
            state_ref[:, gs] = st * elast_x[:, gs] + _dot_tn(b_g, xdd[:, gs])

    z = _dot(u, wz_ref[...])
    y = _rms(ys_ref[...] * _silu(z), gn_ref[...]).astype(BF16)
    o_ref[...] = h + _dot(y, wo_ref[...])


def _ssd(h, ng, in_proj, conv_w, conv_b, dt_bias, a_log, d_skip, norm_g, out_proj, *, tl):
    di, bc, nh = SSD_D_INNER, SSD_BC_DIM, SSD_N_HEADS
    o_x, o_b, o_c, o_dt = di, 2 * di, 2 * di + bc, 2 * di + 2 * bc

    def pad_heads(v):
        return jnp.pad(v, (0, LANES - nh))

    w_dt = jnp.pad(in_proj[:, o_dt:], ((0, 0), (0, LANES - nh)))
    head_of_col = jnp.arange(di, dtype=jnp.int32) // SSD_HEAD_DIM
    expand = (jnp.arange(LANES, dtype=jnp.int32)[:, None] == head_of_col[None, :]).astype(BF16)
    ii = jnp.arange(SSD_CHUNK, dtype=jnp.int32)
    tril = (ii[:, None] >= ii[None, :]).astype(BF16)
    weights = [
        ng.reshape(1, -1),
        in_proj[:, :o_x].astype(BF16), in_proj[:, o_x:o_b].astype(BF16),
        in_proj[:, o_b:o_c].astype(BF16), in_proj[:, o_c:o_dt].astype(BF16),
        w_dt.astype(BF16), w_dt.T.astype(BF16),
        conv_w[:, :di], conv_w[:, di:di + bc], conv_w[:, di + bc:],
        conv_b[None, :di], conv_b[None, di:di + bc], conv_b[None, di + bc:],
        pad_heads(dt_bias)[None, :], pad_heads(dt_bias)[:, None],
        pad_heads(a_log)[None, :], pad_heads(a_log)[:, None],
        jnp.repeat(d_skip, SSD_HEAD_DIM)[None, :],
        norm_g.reshape(1, -1), out_proj.astype(BF16),
        expand, tril, tril.T,
    ]
    scratch = ([pltpu.VMEM((tl + CONV_PAD, LANES), F32)] * ((di + 2 * bc) // LANES) + [
        pltpu.VMEM((tl, D_MODEL), BF16),
        pltpu.VMEM((tl, di), F32), pltpu.VMEM((tl, bc), BF16), pltpu.VMEM((tl, bc), BF16),
        pltpu.VMEM((tl, di), F32), pltpu.VMEM((SSD_D_STATE, di), F32),
    ])
    return _call(functools.partial(_ssd_body, tl=tl), h, weights, scratch, tl, "ssd_mixer")


def _conf_body(h_ref, ng_ref, w1_ref, b1_ref, dww_ref, dwb_ref, lng_ref, lnb_ref,
               w2_ref, b2_ref, o_ref, *scratch, tl):
    pad, taps, rb, d, cc = CONF_PAD, CONF_KERNEL, CONF_ROW_BLOCK, D_MODEL, CONF_CHUNK
    nslab = d // LANES
    slabs, (u_ref, cv_ref) = scratch[:nslab], scratch[nslab:]

    @pl.when(pl.program_id(1) == 0)
    def _():
        for s_ref in slabs:
            s_ref[0:pad, :] = jnp.zeros((pad, LANES), F32)

    u_ref[...] = _rms(h_ref[...], ng_ref[...]).astype(BF16)
    per = cc // LANES
    for c in range(d // cc):
        ca, cg = slice(c * cc, (c + 1) * cc), slice(d + c * cc, d + (c + 1) * cc)
        glu = ((_dot(u_ref[...], w1_ref[:, ca]) + b1_ref[:, ca])
               * _sigmoid(_dot(u_ref[...], w1_ref[:, cg]) + b1_ref[:, cg]))
        for i in range(per):
            s_ref = slabs[c * per + i]
            cs = slice((c * per + i) * LANES, (c * per + i + 1) * LANES)
            s_ref[pad:pad + tl, :] = glu[:, i * LANES:(i + 1) * LANES]
            w_taps = [jnp.broadcast_to(dww_ref[k:k + 1, cs], (rb, LANES)) for k in range(taps)]
            bias = jnp.broadcast_to(dwb_ref[:, cs], (rb, LANES))
            for r0 in range(0, tl, rb):
                acc = bias
                for k in range(taps):
                    off = pad - (taps - 1) + k + r0
                    acc = acc + s_ref[off:off + rb, :] * w_taps[k]
                cv_ref[r0:r0 + rb, cs] = acc
            s_ref[0:pad, :] = s_ref[tl:tl + pad, :]

    v = _silu(_layer_norm(cv_ref[...], lng_ref[...], lnb_ref[...])).astype(BF16)
    o_ref[...] = h_ref[...] + _dot(v, w2_ref[...]) + b2_ref[...]


def _conformer(h, ng, pw1_w, pw1_b, dw_w, dw_b, ln_g, ln_b, pw2_w, pw2_b, *, tl):
    d = D_MODEL
    weights = [
        ng.reshape(1, -1), pw1_w.astype(BF16), pw1_b[None, :],
        dw_w, dw_b[None, :], ln_g[None, :], ln_b[None, :],
        pw2_w.astype(BF16), pw2_b[None, :],
    ]
    scratch = ([pltpu.VMEM((tl + CONF_PAD, LANES), F32)] * (d // LANES)
               + [pltpu.VMEM((tl, d), BF16), pltpu.VMEM((tl, d), F32)])
    return _call(functools.partial(_conf_body, tl=tl), h, weights, scratch, tl, "conformer_conv")


def _scan_stride(tl):
    return (tl // SUBLANES) | 1


def _lru_body(h_ref, ng_ref, wg_ref, wx_ref, bg_ref, bx_ref, cw_ref, cb_ref, gaw_ref, gab_ref,
              gxw_ref, gxb_ref, lam_ref, wo_ref, bo_ref, o_ref, *scratch, tl):
    pad, taps = CONV_PAD, LRU_CONV
    seg, stride = tl // SUBLANES, _scan_stride(tl)
    nslab = LRU_WIDTH // LANES
    per = LRU_BLOCK // LANES
    xslabs, sa, sb = scratch[:nslab], scratch[nslab:2 * nslab], scratch[2 * nslab:3 * nslab]
    u_ref, hprev_ref, y_ref = scratch[3 * nslab:]

    @pl.when(pl.program_id(1) == 0)
    def _():
        hprev_ref[...] = jnp.zeros_like(hprev_ref)
        for s in range(nslab):
            xslabs[s][0:pad, :] = jnp.zeros((pad, LANES), F32)

    u_ref[...] = _rms(h_ref[...], ng_ref[...]).astype(BF16)
    for k in range(LRU_N_BLOCKS):
        bs = slice(k * LRU_BLOCK, (k + 1) * LRU_BLOCK)
        x_raw = _dot(u_ref[...], wx_ref[:, bs]) + bx_ref[:, bs]
        x_parts = []
        for i in range(per):
            s = k * per + i
            cs = slice(s * LANES, (s + 1) * LANES)
            xslabs[s][pad:pad + tl, :] = x_raw[:, i * LANES:(i + 1) * LANES]
            acc = cb_ref[:, cs]
            for t in range(taps):
                off = pad - (taps - 1) + t
                acc = acc + xslabs[s][off:off + tl, :] * cw_ref[t:t + 1, cs]
            xslabs[s][0:pad, :] = xslabs[s][tl:tl + pad, :]
            x_parts.append(acc)
        xr = jnp.concatenate(x_parts, axis=1)
        xb = xr.astype(BF16)
        r = _sigmoid(_dot(xb, gaw_ref[k]) + gab_ref[k])
        ig = _sigmoid(_dot(xb, gxw_ref[k]) + gxb_ref[k])
        log_a = (-LRU_C) * r * _softplus(-lam_ref[:, bs])
        a_blk = jnp.exp(log_a)
        b_blk = jnp.sqrt(jnp.tanh(-log_a) * (1.0 + a_blk * a_blk)) * (ig * xr)
        gate = _gelu_tanh(_dot(u_ref[...], wg_ref[:, bs]) + bg_ref[:, bs])

        for i in range(per):
            s = k * per + i
            cs = slice(s * LANES, (s + 1) * LANES)
            ls = slice(i * LANES, (i + 1) * LANES)
            for r in range(SUBLANES):
                sa[s][r * stride:r * stride + seg, :] = a_blk[r * seg:(r + 1) * seg, ls]
                sb[s][r * stride:r * stride + seg, :] = b_blk[r * seg:(r + 1) * seg, ls]
            h_loc = jnp.zeros((SUBLANES, LANES), F32)
            a_pre = jnp.ones((SUBLANES, LANES), F32)
            for j in range(seg):
                rows = pl.ds(j, SUBLANES, stride=stride)
                a_j = sa[s][rows, :]
                h_loc = a_j * h_loc + sb[s][rows, :]
                a_pre = a_j * a_pre
                sb[s][rows, :] = h_loc
                sa[s][rows, :] = a_pre
            carry = hprev_ref[:, cs]
            for r in range(SUBLANES):
                h_seg = sb[s][r * stride:r * stride + seg, :] + sa[s][r * stride:r * stride + seg, :] * carry
                y_ref[r * seg:(r + 1) * seg, cs] = (gate[r * seg:(r + 1) * seg, ls] * h_seg).astype(BF16)
                carry = a_pre[r:r + 1, :] * carry + h_loc[r:r + 1, :]
            hprev_ref[:, cs] = carry

    o_ref[...] = h_ref[...] + _dot(y_ref[...], wo_ref[...]) + bo_ref[...]


def _rglru(h, ng, in_w, in_b, conv_w, conv_b, ga_w, ga_b, gx_w, gx_b, lam, out_w, out_b, *, tl):
    w = LRU_WIDTH
    weights = [
        ng.reshape(1, -1),
        in_w[:, :w].astype(BF16), in_w[:, w:].astype(BF16), in_b[None, :w], in_b[None, w:],
        conv_w, conv_b[None, :],
        ga_w.astype(BF16), ga_b[:, None, :], gx_w.astype(BF16), gx_b[:, None, :],
        lam[None, :], out_w.astype(BF16), out_b[None, :],
    ]
    nslab = w // LANES
    scratch = ([pltpu.VMEM((tl + CONV_PAD, LANES), F32)] * nslab
               + [pltpu.VMEM((SUBLANES * _scan_stride(tl), LANES), F32)] * (2 * nslab)
               + [pltpu.VMEM((tl, D_MODEL), BF16), pltpu.VMEM((1, w), F32), pltpu.VMEM((tl, w), BF16)])
    return _call(functools.partial(_lru_body, tl=tl), h, weights, scratch, tl, "rglru_block")


def _sgu_body(h_ref, ng_ref, wu_ref, wv_ref, bu_ref, bv_ref, lng_ref, lnb_ref, spw_ref, spb_ref,
              wo_ref, bo_ref, o_ref, zu_ref, vn_ref, gt_ref, *, tl):
    q = SGU_CHUNK
    h = h_ref[...]
    u = _rms(h, ng_ref[...]).astype(BF16)
    zu_ref[...] = _gelu_tanh(_dot(u, wu_ref[...]) + bu_ref[...])
    v = _gelu_tanh(_dot(u, wv_ref[...]) + bv_ref[...])
    vn_ref[...] = _layer_norm(v, lng_ref[...], lnb_ref[...]).astype(BF16)

    causal = (lax.broadcasted_iota(jnp.int32, (q, q), 0) >= lax.broadcasted_iota(jnp.int32, (q, q), 1))
    spb = spb_ref[...]
    for g in range(SGU_GROUPS):
        gs = slice(g * SGU_GROUP_DIM, (g + 1) * SGU_GROUP_DIM)
        w = jnp.where(causal, spw_ref[g], 0.0).astype(BF16)
        bias = spb[:, g:g + 1]
        for c in range(tl // q):
            rows = slice(c * q, (c + 1) * q)
            mixed = _dot(w, vn_ref[rows, gs]) + bias
            gt_ref[rows, gs] = (zu_ref[rows, gs] * mixed).astype(BF16)
    o_ref[...] = h + _dot(gt_ref[...], wo_ref[...]) + bo_ref[...]


def _sgu(h, ng, in_w, in_b, ln_g, ln_b, sp_w, sp_b, out_w, out_b, *, tl):
    hf = SGU_HALF
    weights = [
        ng.reshape(1, -1),
        in_w[:, :hf].astype(BF16), in_w[:, hf:].astype(BF16), in_b[None, :hf], in_b[None, hf:],
        ln_g[None, :], ln_b[None, :], sp_w, sp_b.T,
        out_w.astype(BF16), out_b[None, :],
    ]
    scratch = [pltpu.VMEM((tl, hf), F32), pltpu.VMEM((tl, hf), BF16), pltpu.VMEM((tl, hf), BF16)]
    return _call(functools.partial(_sgu_body, tl=tl), h, weights, scratch, tl, "chunked_sgu")


def _tile(seqlen, want):
    return min(want, seqlen)


TL_SSD = 512
TL_CONF = 1024
TL_LRU = 1024
TL_SGU = 1024
TL_FFN = 512


def kernel(x, norm_mix, norm_ffn, norm_final, a_in_proj, a_conv_w, a_conv_b, a_dt_bias, a_log, a_d_skip, a_norm, a_out_proj, b_pw1_w, b_pw1_b, b_dw_w, b_dw_b, b_ln_g, b_ln_b, b_pw2_w, b_pw2_b, c_in_w, c_in_b, c_conv_w, c_conv_b, c_ga_w, c_ga_b, c_gx_w, c_gx_b, c_lambda, c_out_w, c_out_b, d_in_w, d_in_b, d_ln_g, d_ln_b, d_sp_w, d_sp_b, d_out_w, d_out_b, f_up_w, f_conv_w, f_conv_b, f_down_w):
    depth = norm_mix.shape[0]
    seqlen = x.shape[1]
    h = x
    for i in range(depth):
        kind, j = i % 4, i // 4
        if kind == 0:
            h = _ssd(h, norm_mix[i], a_in_proj[j], a_conv_w[j], a_conv_b[j], a_dt_bias[j], a_log[j],
                     a_d_skip[j], a_norm[j], a_out_proj[j], tl=_tile(seqlen, TL_SSD))
        elif kind == 1:
            h = _conformer(h, norm_mix[i], b_pw1_w[j], b_pw1_b[j], b_dw_w[j], b_dw_b[j], b_ln_g[j],
                           b_ln_b[j], b_pw2_w[j], b_pw2_b[j], tl=_tile(seqlen, TL_CONF))
        elif kind == 2:
            h = _rglru(h, norm_mix[i], c_in_w[j], c_in_b[j], c_conv_w[j], c_conv_b[j], c_ga_w[j],
                       c_ga_b[j], c_gx_w[j], c_gx_b[j], c_lambda[j], c_out_w[j], c_out_b[j],
                       tl=_tile(seqlen, TL_LRU))
        else:
            h = _sgu(h, norm_mix[i], d_in_w[j], d_in_b[j], d_ln_g[j], d_ln_b[j], d_sp_w[j], d_sp_b[j],
                     d_out_w[j], d_out_b[j], tl=_tile(seqlen, TL_SGU))
        h = _ffn(h, norm_ffn[i], f_up_w[i], f_conv_w[i], f_conv_b[i], f_down_w[i], norm_final,
                 final_norm=(i == depth - 1), tl=_tile(seqlen, TL_FFN))
    return h
```

```python
import functools
import math

import jax
import jax.numpy as jnp
from jax import lax
from jax.experimental import pallas as pl
from jax.experimental.pallas import tpu as pltpu

F32 = jnp.float32
BF16 = jnp.bfloat16

RMS_EPS = 1e-6
LN_EPS = 1e-5
LOG2_E = math.log2(math.e)

D_MODEL = 1024

LANES = 128
SUBLANES = 8
VMEM_LIMIT_BYTES = 56 * 1024 * 1024
SSD_D_INNER = 2048
SSD_HEAD_DIM = 64
SSD_N_HEADS = 32
SSD_N_GROUPS = 8
SSD_HEADS_PER_GROUP = 4
SSD_D_STATE = 128
SSD_CONV = 4
SSD_CHUNK = 128
SSD_BC_DIM = 1024
SSD_GROUP_W = SSD_HEADS_PER_GROUP * SSD_HEAD_DIM

CONF_KERNEL = 31
CONF_PAD = 32
CONF_CHUNK = 256
CONF_ROW_BLOCK = 64

LRU_WIDTH = 1280
LRU_BLOCK = 256
LRU_N_BLOCKS = 5
LRU_CONV = 4
LRU_C = 8.0

SGU_CHUNK = 128
SGU_HALF = 2048
SGU_GROUPS = 8
SGU_GROUP_DIM = 256

FFN_HIDDEN = 2816
FFN_CONV = 3
FFN_CHUNK = 256
FFN_ROW_BLOCK = 32

CONV_PAD = SUBLANES


def _dot(a, b):
    return jnp.dot(a, b, preferred_element_type=F32)


def _dot_nt(a, b):
    return lax.dot_general(a, b, (((1,), (1,)), ((), ())), preferred_element_type=F32)


def _dot_tn(a, b):
    return lax.dot_general(a, b, (((0,), (0,)), ((), ())), preferred_element_type=F32)


def _rms(x, g):
    return x * lax.rsqrt(jnp.mean(x * x, axis=-1, keepdims=True) + RMS_EPS) * g


def _layer_norm(x, g, b):
    mu = jnp.mean(x, axis=-1, keepdims=True)
    xc = x - mu
    return xc * lax.rsqrt(jnp.mean(xc * xc, axis=-1, keepdims=True) + LN_EPS) * g + b


def _sigmoid(x):
    return 1.0 / (1.0 + jnp.exp2(x * (-LOG2_E)))


def _silu(x):
    return x * _sigmoid(x)


def _softplus(x):
    return jnp.maximum(x, 0.0) + jnp.log1p(jnp.exp(-jnp.abs(x)))


def _gelu_tanh(x):
    c = math.sqrt(2.0 / math.pi)
    return x * (0.5 * (1.0 + jnp.tanh(c * (x + 0.044715 * (x * x * x)))))


def _split_hi_lo(x):
    hi = x.astype(BF16)
    lo = (x - hi.astype(F32)).astype(BF16)
    return hi, lo


def _shift_rows(y, tail, s):
    rolled = pltpu.roll(y, s, axis=0)
    row = lax.broadcasted_iota(jnp.int32, (SUBLANES, 1), 0)
    head = jnp.where(row < s, pltpu.roll(tail, s, axis=0), rolled[0:SUBLANES, :])
    return jnp.concatenate([head, rolled[SUBLANES:, :]], axis=0)


def _causal_conv(y, tail_ref, w, b, taps, tl):
    tail = tail_ref[...]
    tail_ref[...] = y[tl - CONV_PAD:tl, :]
    acc = y * w[taps - 1:taps, :] + b
    for k in range(taps - 1):
        acc = acc + _shift_rows(y, tail, taps - 1 - k) * w[k:k + 1, :]
    return acc


def _wspec(shape):
    nd = len(shape)
    return pl.BlockSpec(shape, lambda b, l: (0,) * nd, pipeline_mode=pl.Buffered(1))


def _tile_spec(tl):
    return pl.BlockSpec((None, tl, D_MODEL), lambda b, l: (b, l, 0))


def _call(body, x, weights, scratch, tl, name):
    bsz, seqlen, _ = x.shape
    assert seqlen % tl == 0, (seqlen, tl)
    return pl.pallas_call(
        body,
        grid=(bsz, seqlen // tl),
        in_specs=[_tile_spec(tl)] + [_wspec(w.shape) for w in weights],
        out_specs=_tile_spec(tl),
        out_shape=jax.ShapeDtypeStruct(x.shape, x.dtype),
        scratch_shapes=scratch,
        compiler_params=pltpu.CompilerParams(
            dimension_semantics=("arbitrary", "arbitrary"),
            vmem_limit_bytes=VMEM_LIMIT_BYTES),
        name=name,
    )(x, *weights)


def _ffn_body(h_ref, ng_ref, up_ref, cw_ref, cb_ref, dn_ref, fg_ref,
              o_ref, tail_ref, yg0_ref, yv0_ref, yg1_ref, yv1_ref, u_ref, a_ref, *, tl, final_norm):
    pad, taps, rb, hid, hc = CONV_PAD, FFN_CONV, FFN_ROW_BLOCK, FFN_HIDDEN, FFN_CHUNK
    nch, nhalf = hid // hc, hc // LANES

    @pl.when(pl.program_id(1) == 0)
    def _():
        tail_ref[...] = jnp.zeros_like(tail_ref)

    u_ref[...] = _rms(h_ref[...], ng_ref[...]).astype(BF16)
    even_bufs, odd_bufs = (yg0_ref, yv0_ref), (yg1_ref, yv1_ref)

    def up_chunk(j, bufs):
        for part in range(2):
            c0 = part * hid + j * hc
            y = _dot(u_ref[...], up_ref[:, c0:c0 + hc])
            for hf in range(nhalf):
                cs = slice(c0 + hf * LANES, c0 + (hf + 1) * LANES)
                bufs[part][hf, 0:pad, :] = tail_ref[:, cs]
                bufs[part][hf, pad:pad + tl, :] = y[:, hf * LANES:(hf + 1) * LANES]
                tail_ref[:, cs] = bufs[part][hf, tl:tl + pad, :]

    def act_chunk(j, bufs):
        for hf in range(nhalf):
            for r0 in range(0, tl, rb):
                conv = []
                for part in range(2):
                    c0 = part * hid + j * hc + hf * LANES
                    acc = cb_ref[:, c0:c0 + LANES]
                    for k in range(taps):
                        off = pad - (taps - 1) + k + r0
                        acc = acc + bufs[part][hf, off:off + rb, :] * cw_ref[k:k + 1, c0:c0 + LANES]
                    conv.append(acc)
                a_ref[r0:r0 + rb, j * hc + hf * LANES:j * hc + (hf + 1) * LANES] = (
                    _silu(conv[0]) * conv[1]).astype(BF16)

    for j in range(nch):
        bufs = (even_bufs, odd_bufs)[j % 2]
        up_chunk(j, bufs)
        act_chunk(j, bufs)

    out = h_ref[...] + _dot(a_ref[...], dn_ref[...])
    if final_norm:
        out = _rms(out, fg_ref[...])
    o_ref[...] = out


def _ffn(h, ng, up_w, conv_w, conv_b, down_w, final_g, *, final_norm, tl):
    hid, hc = FFN_HIDDEN, FFN_CHUNK
    weights = [ng.reshape(1, -1), up_w.astype(BF16), conv_w, conv_b[None, :], down_w.astype(BF16),
               final_g.reshape(1, -1)]
    scratch = ([pltpu.VMEM((CONV_PAD, 2 * hid), F32)]
               + [pltpu.VMEM((hc // LANES, tl + CONV_PAD, LANES), F32)] * 4
               + [pltpu.VMEM((tl, D_MODEL), BF16), pltpu.VMEM((tl, hid), BF16)])
    body = functools.partial(_ffn_body, tl=tl, final_norm=final_norm)
    return _call(body, h, weights, scratch, tl, "conv_ffn")


def _ssd_body(h_ref, ng_ref, wz_ref, wx_ref, wb_ref, wc_ref, wdt_ref, wdtt_ref,
              cwx_ref, cwb_ref, cwc_ref, cbx_ref, cbb_ref, cbc_ref,
              dtb_r_ref, dtb_c_ref, alog_r_ref, alog_c_ref, dskip_ref, gn_ref, wo_ref,
              expand_ref, tril_ref, triu_ref,
              o_ref, *scratch, tl):
    q, pad, taps = SSD_CHUNK, CONV_PAD, SSD_CONV
    nx, nbc = SSD_D_INNER // LANES, SSD_BC_DIM // LANES
    xslabs, bslabs, cslabs = scratch[:nx], scratch[nx:nx + nbc], scratch[nx + nbc:nx + 2 * nbc]
    u_ref, xs_ref, bs_ref, cs_ref, ys_ref, state_ref = scratch[nx + 2 * nbc:]

    @pl.when(pl.program_id(1) == 0)
    def _():
        for slab in scratch[:nx + 2 * nbc]:
            slab[0:pad, :] = jnp.zeros((pad, LANES), F32)
        state_ref[...] = jnp.zeros_like(state_ref)

    h = h_ref[...]
    u_ref[...] = _rms(h, ng_ref[...]).astype(BF16)
    u = u_ref[...]

    def conv_silu(w_ref, cw_ref, cb_ref, slabs, out_ref):
        step = 2 * LANES
        for c0 in range(0, w_ref.shape[1], step):
            y = _dot(u_ref[...], w_ref[:, c0:c0 + step])
            for i in range(step // LANES):
                s = c0 // LANES + i
                cs = slice(s * LANES, (s + 1) * LANES)
                slabs[s][pad:pad + tl, :] = y[:, i * LANES:(i + 1) * LANES]
                acc = cb_ref[:, cs]
                for t in range(taps):
                    off = pad - (taps - 1) + t
                    acc = acc + slabs[s][off:off + tl, :] * cw_ref[t:t + 1, cs]
                slabs[s][0:pad, :] = slabs[s][tl:tl + pad, :]
                out_ref[:, cs] = _silu(acc).astype(out_ref.dtype)

    conv_silu(wx_ref, cwx_ref, cbx_ref, xslabs, xs_ref)
    conv_silu(wb_ref, cwb_ref, cbb_ref, bslabs, bs_ref)
    conv_silu(wc_ref, cwc_ref, cbc_ref, cslabs, cs_ref)

    head_r = lax.broadcasted_iota(jnp.int32, (1, LANES), 1) < SSD_N_HEADS
    head_c = lax.broadcasted_iota(jnp.int32, (LANES, 1), 0) < SSD_N_HEADS
    a_r = jnp.where(head_r, -jnp.exp(alog_r_ref[...]), 0.0)
    a_c = jnp.where(head_c, -jnp.exp(alog_c_ref[...]), 0.0)
    dt_all = _softplus(_dot(u, wdt_ref[...]) + dtb_r_ref[...])
    dtt_all = _softplus(_dot_nt(wdtt_ref[...], u) + dtb_c_ref[...])

    causal = (lax.broadcasted_iota(jnp.int32, (q, q), 0) >= lax.broadcasted_iota(jnp.int32, (q, q), 1))
    lane_head = lax.broadcasted_iota(jnp.int32, (1, SSD_GROUP_W), 1) // SSD_HEAD_DIM
    row16 = lax.broadcasted_iota(jnp.int32, (16, LANES), 0)
    expand = expand_ref[...]
    tril = tril_ref[...]
    triu = triu_ref[...]
    dskip = dskip_ref[...]

    for c in range(tl // q):
        rows = slice(c * q, (c + 1) * q)
        dt = dt_all[rows, :]
        dtt = dtt_all[:, rows]
        hi, lo = _split_hi_lo(dt * a_r)
        acs = _dot(tril, hi) + _dot(tril, lo)
        hi_t, lo_t = _split_hi_lo(dtt * a_c)
        acs_t = _dot(hi_t, triu) + _dot(lo_t, triu)
        eacs = jnp.exp(acs)
        acs_last = acs[q - 1:q, :]
        dend = jnp.exp(acs_last - acs) * dt
        el_hi, el_lo = _split_hi_lo(jnp.exp(acs_last))
        el = jnp.where(row16 == 0, el_hi.astype(F32), jnp.where(row16 == 1, el_lo.astype(F32), 0.0))
        el_x = _dot(el.astype(BF16), expand)
        elast_x = el_x[0:1, :] + el_x[1:2, :]
        dend_x = _dot(dend.astype(BF16), expand)

        x_c = xs_ref[rows, :]
        x_b = x_c.astype(BF16)
        xdd = (x_c * dend_x).astype(BF16)
        b_c = bs_ref[rows, :]
        c_c = cs_ref[rows, :]
        for g in range(SSD_N_GROUPS):
            gs = slice(g * SSD_GROUP_W, (g + 1) * SSD_GROUP_W)
            ns = slice(g * SSD_D_STATE, (g + 1) * SSD_D_STATE)
            b_g = b_c[:, ns]
            c_g = c_c[:, ns]
            cb = _dot_nt(c_g, b_g)
            st = state_ref[:, gs]
            st_b = st.astype(BF16)
            x_g = x_b[:, gs]
            c_g32 = c_g.astype(F32)
            lhs = []
            for j in range(SSD_HEADS_PER_GROUP):
                hd = g * SSD_HEADS_PER_GROUP + j
                seg = acs[:, hd:hd + 1] - acs_t[hd:hd + 1, :]
                decay = jnp.exp(jnp.where(causal, seg, -jnp.inf))
                scores = (cb * decay * dtt[hd:hd + 1, :]).astype(BF16)
                c_scaled = (c_g32 * eacs[:, hd:hd + 1]).astype(BF16)
                lhs.append(jnp.concatenate([scores, c_scaled], axis=1))
            prod = _dot(jnp.concatenate(lhs, axis=0), jnp.concatenate([x_g, st_b], axis=0))
            y_heads = prod[(SSD_HEADS_PER_GROUP - 1) * q:, :]
            for j in range(SSD_HEADS_PER_GROUP - 2, -1, -1):
                y_heads = jnp.where(lane_head == j, prod[j * q:(j + 1) * q, :], y_heads)
            ys_ref[rows, gs] = x_c[:, gs] * dskip[:, gs] + y_heads
            state_ref[:, gs] = st * elast_x[:, gs] + _dot_tn(b_g, xdd[:, gs])

    z = _dot(u, wz_ref[...])
    y = _rms(ys_ref[...] * _silu(z), gn_ref[...]).astype(BF16)
    o_ref[...] = h + _dot(y, wo_ref[...])


def _ssd(h, ng, in_proj, conv_w, conv_b, dt_bias, a_log, d_skip, norm_g, out_proj, *, tl):
    di, bc, nh = SSD_D_INNER, SSD_BC_DIM, SSD_N_HEADS
    o_x, o_b, o_c, o_dt = di, 2 * di, 2 * di + bc, 2 * di + 2 * bc

    def pad_heads(v):
        return jnp.pad(v, (0, LANES - nh))

    w_dt = jnp.pad(in_proj[:, o_dt:], ((0, 0), (0, LANES - nh)))
    head_of_col = jnp.arange(di, dtype=jnp.int32) // SSD_HEAD_DIM
    expand = (jnp.arange(LANES, dtype=jnp.int32)[:, None] == head_of_col[None, :]).astype(BF16)
    ii = jnp.arange(SSD_CHUNK, dtype=jnp.int32)
    tril = (ii[:, None] >= ii[None, :]).astype(BF16)
    weights = [
        ng.reshape(1, -1),
        in_proj[:, :o_x].astype(BF16), in_proj[:, o_x:o_b].astype(BF16),
        in_proj[:, o_b:o_c].astype(BF16), in_proj[:, o_c:o_dt].astype(BF16),
        w_dt.astype(BF16), w_dt.T.astype(BF16),
        conv_w[:, :di], conv_w[:, di:di + bc], conv_w[:, di + bc:],
        conv_b[None, :di], conv_b[None, di:di + bc], conv_b[None, di + bc:],
        pad_heads(dt_bias)[None, :], pad_heads(dt_bias)[:, None],
        pad_heads(a_log)[None, :], pad_heads(a_log)[:, None],
        jnp.repeat(d_skip, SSD_HEAD_DIM)[None, :],
        norm_g.reshape(1, -1), out_proj.astype(BF16),
        expand, tril, tril.T,
    ]
    scratch = ([pltpu.VMEM((tl + CONV_PAD, LANES), F32)] * ((di + 2 * bc) // LANES) + [
        pltpu.VMEM((tl, D_MODEL), BF16),
        pltpu.VMEM((tl, di), F32), pltpu.VMEM((tl, bc), BF16), pltpu.VMEM((tl, bc), BF16),
        pltpu.VMEM((tl, di), F32), pltpu.VMEM((SSD_D_STATE, di), F32),
    ])
    return _call(functools.partial(_ssd_body, tl=tl), h, weights, scratch, tl, "ssd_mixer")


def _conf_body(h_ref, ng_ref, w1_ref, b1_ref, dww_ref, dwb_ref, lng_ref, lnb_ref,
               w2_ref, b2_ref, o_ref, *scratch, tl):
    pad, taps, rb, d, cc = CONF_PAD, CONF_KERNEL, CONF_ROW_BLOCK, D_MODEL, CONF_CHUNK
    nslab = d // LANES
    slabs, (u_ref, cv_ref) = scratch[:nslab], scratch[nslab:]

    @pl.when(pl.program_id(1) == 0)
    def _():
        for s_ref in slabs:
            s_ref[0:pad, :] = jnp.zeros((pad, LANES), F32)

    u_ref[...] = _rms(h_ref[...], ng_ref[...]).astype(BF16)
    per = cc // LANES
    for c in range(d // cc):
        ca, cg = slice(c * cc, (c + 1) * cc), slice(d + c * cc, d + (c + 1) * cc)
        glu = ((_dot(u_ref[...], w1_ref[:, ca]) + b1_ref[:, ca])
               * _sigmoid(_dot(u_ref[...], w1_ref[:, cg]) + b1_ref[:, cg]))
        for i in range(per):
            s_ref = slabs[c * per + i]
            cs = slice((c * per + i) * LANES, (c * per + i + 1) * LANES)
            s_ref[pad:pad + tl, :] = glu[:, i * LANES:(i + 1) * LANES]
            w_taps = [jnp.broadcast_to(dww_ref[k:k + 1, cs], (rb, LANES)) for k in range(taps)]
            bias = jnp.broadcast_to(dwb_ref[:, cs], (rb, LANES))
            for r0 in range(0, tl, rb):
                acc = bias
                for k in range(taps):
                    off = pad - (taps - 1) + k + r0
                    acc = acc + s_ref[off:off + rb, :] * w_taps[k]
                cv_ref[r0:r0 + rb, cs] = acc
            s_ref[0:pad, :] = s_ref[tl:tl + pad, :]

    v = _silu(_layer_norm(cv_ref[...], lng_ref[...], lnb_ref[...])).astype(BF16)
    o_ref[...] = h_ref[...] + _dot(v, w2_ref[...]) + b2_ref[...]


def _conformer(h, ng, pw1_w, pw1_b, dw_w, dw_b, ln_g, ln_b, pw2_w, pw2_b, *, tl):
    d = D_MODEL
    weights = [
        ng.reshape(1, -1), pw1_w.astype(BF16), pw1_b[None, :],
        dw_w, dw_b[None, :], ln_g[None, :], ln_b[None, :],
        pw2_w.astype(BF16), pw2_b[None, :],
    ]
    scratch = ([pltpu.VMEM((tl + CONF_PAD, LANES), F32)] * (d // LANES)
               + [pltpu.VMEM((tl, d), BF16), pltpu.VMEM((tl, d), F32)])
    return _call(functools.partial(_conf_body, tl=tl), h, weights, scratch, tl, "conformer_conv")


def _scan_stride(tl):
    return (tl // SUBLANES) | 1


def _lru_body(h_ref, ng_ref, wg_ref, wx_ref, bg_ref, bx_ref, cw_ref, cb_ref, gaw_ref, gab_ref,
              gxw_ref, gxb_ref, lam_ref, wo_ref, bo_ref, o_ref, *scratch, tl):
    pad, taps = CONV_PAD, LRU_CONV
    seg, stride = tl // SUBLANES, _scan_stride(tl)
    nslab = LRU_WIDTH // LANES
    per = LRU_BLOCK // LANES
    xslabs, sa, sb = scratch[:nslab], scratch[nslab:2 * nslab], scratch[2 * nslab:3 * nslab]
    u_ref, hprev_ref, y_ref = scratch[3 * nslab:]

    @pl.when(pl.program_id(1) == 0)
    def _():
        hprev_ref[...] = jnp.zeros_like(hprev_ref)
        for s in range(nslab):
            xslabs[s][0:pad, :] = jnp.zeros((pad, LANES), F32)

    u_ref[...] = _rms(h_ref[...], ng_ref[...]).astype(BF16)
    for k in range(LRU_N_BLOCKS):
        bs = slice(k * LRU_BLOCK, (k + 1) * LRU_BLOCK)
        x_raw = _dot(u_ref[...], wx_ref[:, bs]) + bx_ref[:, bs]
        x_parts = []
        for i in range(per):
            s = k * per + i
            cs = slice(s * LANES, (s + 1) * LANES)
            xslabs[s][pad:pad + tl, :] = x_raw[:, i * LANES:(i + 1) * LANES]
            acc = cb_ref[:, cs]
            for t in range(taps):
                off = pad - (taps - 1) + t
                acc = acc + xslabs[s][off:off + tl, :] * cw_ref[t:t + 1, cs]
            xslabs[s][0:pad, :] = xslabs[s][tl:tl + pad, :]
            x_parts.append(acc)
        xr = jnp.concatenate(x_parts, axis=1)
        xb = xr.astype(BF16)
        r = _sigmoid(_dot(xb, gaw_ref[k]) + gab_ref[k])
        ig = _sigmoid(_dot(xb, gxw_ref[k]) + gxb_ref[k])
        log_a = (-LRU_C) * r * _softplus(-lam_ref[:, bs])
        a_blk = jnp.exp(log_a)
        b_blk = jnp.sqrt(jnp.tanh(-log_a) * (1.0 + a_blk * a_blk)) * (ig * xr)
        gate = _gelu_tanh(_dot(u_ref[...], wg_ref[:, bs]) + bg_ref[:, bs])

        for i in range(per):
            s = k * per + i
            cs = slice(s * LANES, (s + 1) * LANES)
            ls = slice(i * LANES, (i + 1) * LANES)
            for r in range(SUBLANES):
                sa[s][r * stride:r * stride + seg, :] = a_blk[r * seg:(r + 1) * seg, ls]
                sb[s][r * stride:r * stride + seg, :] = b_blk[r * seg:(r + 1) * seg, ls]
            h_loc = jnp.zeros((SUBLANES, LANES), F32)
            a_pre = jnp.ones((SUBLANES, LANES), F32)
            for j in range(seg):
                rows = pl.ds(j, SUBLANES, stride=stride)
                a_j = sa[s][rows, :]
                h_loc = a_j * h_loc + sb[s][rows, :]
                a_pre = a_j * a_pre
                sb[s][rows, :] = h_loc
                sa[s][rows, :] = a_pre
            carry = hprev_ref[:, cs]
            for r in range(SUBLANES):
                h_seg = sb[s][r * stride:r * stride + seg, :] + sa[s][r * stride:r * stride + seg, :] * carry
                y_ref[r * seg:(r + 1) * seg, cs] = (gate[r * seg:(r + 1) * seg, ls] * h_seg).astype(BF16)
                carry = a_pre[r:r + 1, :] * carry + h_loc[r:r + 1, :]
            hprev_ref[:, cs] = carry

    o_ref[...] = h_ref[...] + _dot(y_ref[...], wo_ref[...]) + bo_ref[...]


def _rglru(h, ng, in_w, in_b, conv_w, conv_b, ga_w, ga_b, gx_w, gx_b, lam, out_w, out_b, *, tl):
    w = LRU_WIDTH
    weights = [
        ng.reshape(1, -1),
        in_w[:, :w].astype(BF16), in_w[:, w:].astype(BF16), in_b[None, :w], in_b[None, w:],
        conv_w, conv_b[None, :],
        ga_w.astype(BF16), ga_b[:, None, :], gx_w.astype(BF16), gx_b[:, None, :],
        lam[None, :], out_w.astype(BF16), out_b[None, :],
    ]
    nslab = w // LANES
    scratch = ([pltpu.VMEM((tl + CONV_PAD, LANES), F32)] * nslab
               + [pltpu.VMEM((SUBLANES * _scan_stride(tl), LANES), F32)] * (2 * nslab)
               + [pltpu.VMEM((tl, D_MODEL), BF16), pltpu.VMEM((1, w), F32), pltpu.VMEM((tl, w), BF16)])
    return _call(functools.partial(_lru_body, tl=tl), h, weights, scratch, tl, "rglru_block")


def _sgu_body(h_ref, ng_ref, wu_ref, wv_ref, bu_ref, bv_ref, lng_ref, lnb_ref, spw_ref, spb_ref,
              wo_ref, bo_ref, o_ref, u_ref, v_ref, vn_ref, gt_ref, *, tl):
    q, gd = SGU_CHUNK, SGU_GROUP_DIM
    u_ref[...] = _rms(h_ref[...], ng_ref[...]).astype(BF16)
    for g in range(SGU_GROUPS):
        gs = slice(g * gd, (g + 1) * gd)
        v_ref[:, gs] = _gelu_tanh(_dot(u_ref[...], wv_ref[:, gs]) + bv_ref[:, gs])
    vn_ref[...] = _layer_norm(v_ref[...], lng_ref[...], lnb_ref[...]).astype(BF16)

    causal = (lax.broadcasted_iota(jnp.int32, (q, q), 0) >= lax.broadcasted_iota(jnp.int32, (q, q), 1))
    spb = spb_ref[...]
    for g in range(SGU_GROUPS):
        gs = slice(g * gd, (g + 1) * gd)
        zu = _gelu_tanh(_dot(u_ref[...], wu_ref[:, gs]) + bu_ref[:, gs])
        w = jnp.where(causal, spw_ref[g], 0.0).astype(BF16)
        bias = spb[:, g:g + 1]
        for c in range(tl // q):
            rows = slice(c * q, (c + 1) * q)
            mixed = _dot(w, vn_ref[rows, gs]) + bias
            gt_ref[rows, gs] = (zu[rows, :] * mixed).astype(BF16)
    o_ref[...] = h_ref[...] + _dot(gt_ref[...], wo_ref[...]) + bo_ref[...]


def _sgu(h, ng, in_w, in_b, ln_g, ln_b, sp_w, sp_b, out_w, out_b, *, tl):
    hf = SGU_HALF
    weights = [
        ng.reshape(1, -1),
        in_w[:, :hf].astype(BF16), in_w[:, hf:].astype(BF16), in_b[None, :hf], in_b[None, hf:],
        ln_g[None, :], ln_b[None, :], sp_w, sp_b.T,
        out_w.astype(BF16), out_b[None, :],
    ]
    scratch = [pltpu.VMEM((tl, D_MODEL), BF16), pltpu.VMEM((tl, hf), F32),
               pltpu.VMEM((tl, hf), BF16), pltpu.VMEM((tl, hf), BF16)]
    return _call(functools.partial(_sgu_body, tl=tl), h, weights, scratch, tl, "chunked_sgu")


def _tile(seqlen, want):
    return min(want, seqlen)


TL_SSD = 512
TL_CONF = 1024
TL_LRU = 1024
TL_SGU = 1024
TL_FFN = 512


def kernel(x, norm_mix, norm_ffn, norm_final, a_in_proj, a_conv_w, a_conv_b, a_dt_bias, a_log, a_d_skip, a_norm, a_out_proj, b_pw1_w, b_pw1_b, b_dw_w, b_dw_b, b_ln_g, b_ln_b, b_pw2_w, b_pw2_b, c_in_w, c_in_b, c_conv_w, c_conv_b, c_ga_w, c_ga_b, c_gx_w, c_gx_b, c_lambda, c_out_w, c_out_b, d_in_w, d_in_b, d_ln_g, d_ln_b, d_sp_w, d_sp_b, d_out_w, d_out_b, f_up_w, f_conv_w, f_conv_b, f_down_w):
    depth = norm_mix.shape[0]
    seqlen = x.shape[1]
    h = x
    for i in range(depth):
        kind, j = i % 4, i // 4
        if kind == 0:
            h = _ssd(h, norm_mix[i], a_in_proj[j], a_conv_w[j], a_conv_b[j], a_dt_bias[j], a_log[j],
                     a_d_skip[j], a_norm[j], a_out_proj[j], tl=_tile(seqlen, TL_SSD))
        elif kind == 1:
            h = _conformer(h, norm_mix[i], b_pw1_w[j], b_pw1_b[j], b_dw_w[j], b_dw_b[j], b_ln_g[j],
                           b_ln_b[j], b_pw2_w[j], b_pw2_b[j], tl=_tile(seqlen, TL_CONF))
        elif kind == 2:
            h = _rglru(h, norm_mix[i], c_in_w[j], c_in_b[j], c_conv_w[j], c_conv_b[j], c_ga_w[j],
                       c_ga_b[j], c_gx_w[j], c_gx_b[j], c_lambda[j], c_out_w[j], c_out_b[j],
                       tl=_tile(seqlen, TL_LRU))
        else:
            h = _sgu(h, norm_mix[i], d_in_w[j], d_in_b[j], d_ln_g[j], d_ln_b[j], d_sp_w[j], d_sp_b[j],
                     d_out_w[j], d_out_b[j], tl=_tile(seqlen, TL_SGU))
        h = _ffn(h, norm_ffn[i], f_up_w[i], f_conv_w[i], f_conv_b[i], f_down_w[i], norm_final,
                 final_norm=(i == depth - 1), tl=_tile(seqlen, TL_FFN))
    return h
```
